```python
import jax, jax.numpy as jnp
from jax import lax
import numpy as np

D_MODEL = 1024
BATCH = 8
SEQ = 8192
DEPTH = 1
DEC_BATCH = 8
DEC_SEQ = 2048
PAST_LEN = 128

GRID_W = 64
POOL_WIDTH = D_MODEL // 2
POOL_GROUPS = 4
POOL_GROUP_W = POOL_WIDTH // POOL_GROUPS
POOL_WINDOWS = (2, 4, 8, 16)
NA_HEADS = 8
NA_HEAD_DIM = 64
NA_WIDTH = NA_HEADS * NA_HEAD_DIM
NA_ROWS_MAX = 8
NA_COLS = 16
NA_KEY_COLS = 2 * NA_COLS
N_COL_BLOCKS = GRID_W // NA_COLS
D_FF = 4 * D_MODEL
N_BRANCH = 2
IN_WIDTH = POOL_WIDTH + 3 * NA_WIDTH + N_BRANCH * D_MODEL
RMS_EPS = 1e-6
NEG_INF = -1e30

kernel_name = "hybrid_pool_natten_encoder"


def _rmsnorm(x, g):
    xf = x.astype(jnp.float32)
    var = jnp.mean(xf * xf, axis=-1, keepdims=True)
    return (xf * lax.rsqrt(var + RMS_EPS) * g.astype(jnp.float32)).astype(x.dtype)


def _multiscale_pool(p, w_grp, scale):
    b, s, _ = p.shape
    pf = p.astype(jnp.float32)
    csum = jnp.concatenate([jnp.zeros((b, 1, POOL_WIDTH), jnp.float32), jnp.cumsum(pf, axis=1)], axis=1)
    t = np.arange(s)
    outs = []
    for g, w in enumerate(POOL_WINDOWS):
        lo = np.clip(t - w // 2, 0, s)
        hi = np.clip(t + w // 2, 0, s)
        cnt = jnp.asarray(hi - lo, jnp.float32)[None, :, None]
        sl = slice(g * POOL_GROUP_W, (g + 1) * POOL_GROUP_W)
        cg = csum[..., sl]
        mean = (jnp.take(cg, jnp.asarray(hi), axis=1) - jnp.take(cg, jnp.asarray(lo), axis=1)) / cnt
        outs.append(mean - pf[..., sl])
    pooled = jnp.stack(outs, axis=2).astype(p.dtype)
    mixed = jnp.einsum('bsgc,gcd->bsgd', pooled, w_grp).reshape(b, s, POOL_WIDTH)
    return mixed * scale


def _na_tables(rows):
    kr = min(NA_ROWS_MAX, rows)
    r = np.arange(rows)
    rs = np.clip(r - kr // 2, 0, rows - kr)
    row_idx = rs[:, None] + np.arange(kr)[None, :]
    dr = row_idx - r[:, None]
    c0 = np.arange(N_COL_BLOCKS) * NA_COLS
    kcs = np.clip(c0 - NA_COLS // 2, 0, GRID_W - NA_KEY_COLS)
    col_idx = kcs[:, None] + np.arange(NA_KEY_COLS)[None, :]
    qcol = c0[:, None] + np.arange(NA_COLS)[None, :]
    cs = np.clip(qcol - NA_COLS // 2, 0, GRID_W - NA_COLS)
    kc = col_idx[:, None, :]
    dc = kc - qcol[:, :, None]
    col_valid = (kc >= cs[..., None]) & (kc < cs[..., None] + NA_COLS)
    return kr, row_idx, dr, col_idx, dc, col_valid


def _neighbourhood_attention(q, k, v, rpb):
    b, s, _ = q.shape
    rows = s // GRID_W
    kr, row_idx, dr, col_idx, dc, col_valid = _na_tables(rows)
    dr_i = jnp.asarray(dr + NA_ROWS_MAX - 1)[:, None, None, :, None]
    dc_i = jnp.asarray(np.clip(dc + NA_COLS - 1, 0, 2 * NA_COLS - 2))[None, :, :, None, :]
    bias = rpb.astype(jnp.float32)[:, dr_i, dc_i]
    bias = jnp.where(jnp.asarray(col_valid)[None, None, :, :, None, :], bias, NEG_INF)
    row_idx_j = jnp.asarray(row_idx)
    col_idx_j = jnp.asarray(col_idx)
    scale = NA_HEAD_DIM ** -0.5
    grid = (b, rows, GRID_W, NA_HEADS, NA_HEAD_DIM)

    def one(args):
        qe, ke, ve = args
        qb = qe.reshape(rows, N_COL_BLOCKS, NA_COLS, NA_HEADS, NA_HEAD_DIM)
        kb = ke[row_idx_j][:, :, col_idx_j]
        vb = ve[row_idx_j][:, :, col_idx_j]
        sc = jnp.einsum('rnqhd,rinjhd->hrnqij', qb, kb,
                        preferred_element_type=jnp.float32) * scale + bias
        sh = sc.shape
        pr = jax.nn.softmax(sc.reshape(sh[:4] + (kr * NA_KEY_COLS,)), axis=-1).reshape(sh)
        o = jnp.einsum('hrnqij,rinjhd->rnqhd', pr.astype(ve.dtype), vb)
        return o.reshape(s, NA_WIDTH)

    return lax.map(one, (q.reshape(grid), k.reshape(grid), v.reshape(grid)))


def _layer(x, norm_mix, w_in, b_gate, w_pool_grp, pool_scale, w_pool_proj, rpb,
           w_na_proj, w_out, norm_mlp, w_ff1, w_ff2):
    xn = _rmsnorm(x, norm_mix)
    z = xn @ w_in
    o1 = POOL_WIDTH
    o2 = o1 + NA_WIDTH
    o3 = o2 + NA_WIDTH
    o4 = o3 + NA_WIDTH
    p, q, k, v, g = z[..., :o1], z[..., o1:o2], z[..., o2:o3], z[..., o3:o4], z[..., o4:]
    gates = jax.nn.sigmoid(g + b_gate)
    g_pool, g_na = gates[..., :D_MODEL], gates[..., D_MODEL:]
    pool_out = _multiscale_pool(p, w_pool_grp, pool_scale) @ w_pool_proj
    na_out = _neighbourhood_attention(q, k, v, rpb) @ w_na_proj
    x = x + (g_pool * pool_out + g_na * na_out) @ w_out
    xn = _rmsnorm(x, norm_mlp)
    hid = jnp.square(jax.nn.relu(xn @ w_ff1))
    return x + hid @ w_ff2


def _trunk(x, norm_mix, w_in, b_gate, w_pool_grp, pool_scale, w_pool_proj, rpb,
           w_na_proj, w_out, norm_mlp, w_ff1, w_ff2, norm_final):
    for l in range(DEPTH):
        x = _layer(x, norm_mix[l], w_in[l], b_gate[l], w_pool_grp[l], pool_scale[l],
                   w_pool_proj[l], rpb[l], w_na_proj[l], w_out[l], norm_mlp[l],
                   w_ff1[l], w_ff2[l])
    return _rmsnorm(x, norm_final)


def setup_inputs(seed: int = 0) -> dict:
    key = jax.random.key(seed)
    ks = jax.random.split(key, 16)
    f32 = jnp.float32
    nrm = lambda k, shape, s: jax.random.normal(k, shape, f32) * s
    return {
        "x_prompt": nrm(ks[0], (BATCH, SEQ, D_MODEL), 1.0),
        "x_sample": nrm(ks[1], (DEC_BATCH, DEC_SEQ, D_MODEL), 1.0),
        "norm_mix": 1.0 + nrm(ks[2], (DEPTH, D_MODEL), 0.02),
        "w_in": nrm(ks[3], (DEPTH, D_MODEL, IN_WIDTH), D_MODEL ** -0.5),
        "b_gate": nrm(ks[4], (DEPTH, N_BRANCH * D_MODEL), 0.1),
        "w_pool_grp": nrm(ks[5], (DEPTH, POOL_GROUPS, POOL_GROUP_W, POOL_GROUP_W), POOL_GROUP_W ** -0.5),
        "pool_scale": 1.0 + nrm(ks[6], (DEPTH, POOL_WIDTH), 0.02),
        "w_pool_proj": nrm(ks[7], (DEPTH, POOL_WIDTH, D_MODEL), POOL_WIDTH ** -0.5),
        "rpb": nrm(ks[8], (DEPTH, NA_HEADS, 2 * NA_ROWS_MAX - 1, 2 * NA_COLS - 1), 0.1),
        "w_na_proj": nrm(ks[9], (DEPTH, NA_WIDTH, D_MODEL), NA_WIDTH ** -0.5),
        "w_out": nrm(ks[10], (DEPTH, D_MODEL, D_MODEL), D_MODEL ** -0.5),
        "norm_mlp": 1.0 + nrm(ks[11], (DEPTH, D_MODEL), 0.02),
        "w_ff1": nrm(ks[12], (DEPTH, D_MODEL, D_FF), D_MODEL ** -0.5),
        "w_ff2": nrm(ks[13], (DEPTH, D_FF, D_MODEL), D_FF ** -0.5),
        "norm_final": 1.0 + nrm(ks[14], (D_MODEL,), 0.02),
    }


def reference(x_prompt, x_sample, norm_mix, w_in, b_gate, w_pool_grp, pool_scale, w_pool_proj,
              rpb, w_na_proj, w_out, norm_mlp, w_ff1, w_ff2, norm_final):
    y_prompt = _trunk(x_prompt, norm_mix, w_in, b_gate, w_pool_grp, pool_scale, w_pool_proj, rpb,
                      w_na_proj, w_out, norm_mlp, w_ff1, w_ff2, norm_final)
    y_sample = _trunk(x_sample, norm_mix, w_in, b_gate, w_pool_grp, pool_scale, w_pool_proj, rpb,
                      w_na_proj, w_out, norm_mlp, w_ff1, w_ff2, norm_final)
    return (y_prompt, y_sample)
```

```python
import functools

import numpy as np
import jax
import jax.numpy as jnp
from jax import lax
from jax.experimental import pallas as pl
from jax.experimental.pallas import tpu as pltpu

D_MODEL = 1024
GRID_W = 64
POOL_WIDTH = 512
POOL_GROUPS = 4
POOL_GROUP_W = POOL_WIDTH // POOL_GROUPS
POOL_WINDOWS = (2, 4, 8, 16)
POOL_HALO = max(POOL_WINDOWS) // 2
NA_HEADS = 8
NA_HEAD_DIM = 64
NA_WIDTH = NA_HEADS * NA_HEAD_DIM
NA_ROWS = 8
NA_COLS = 16
NA_KEYS = NA_ROWS * GRID_W
HEAD_PAIRS = NA_HEADS // 2
N_BIAS_TILES = 2 * NA_ROWS - 2
D_FF = 4 * D_MODEL
FF_CHUNK = 1024
GATE_WIDTH = 2 * D_MODEL
IN_WIDTH = POOL_WIDTH + 3 * NA_WIDTH + GATE_WIDTH
RMS_EPS = 1e-6
NEG_INF = -1e30
QK_SCALE = NA_HEAD_DIM ** -0.5

LANES = 128
VMEM_LIMIT_BYTES = 56 * 1024 * 1024

_BF16 = jnp.bfloat16
_F32 = jnp.float32


def _rms(x, g):
    var = jnp.mean(x * x, axis=-1, keepdims=True)
    return x * lax.rsqrt(var + RMS_EPS) * g


def _const_spec(shape):
    zeros = (0,) * len(shape)
    return pl.BlockSpec(shape, lambda *_: zeros, pipeline_mode=pl.Buffered(1))


def _inproj_kernel(x_ref, g_ref, w_ref, b_ref, p_ref, q_ref, k_ref, v_ref, gate_ref):
    xn = _rms(x_ref[...], g_ref[...]).astype(_BF16)

    def proj(lo, width):
        return jnp.dot(xn, w_ref[:, lo:lo + width], preferred_element_type=_F32)

    p_ref[...] = proj(0, POOL_WIDTH)
    q_ref[...] = (proj(POOL_WIDTH, NA_WIDTH) * QK_SCALE).astype(_BF16)
    k_ref[...] = proj(POOL_WIDTH + NA_WIDTH, NA_WIDTH).astype(_BF16)
    v_ref[...] = proj(POOL_WIDTH + 2 * NA_WIDTH, NA_WIDTH).astype(_BF16)
    g0 = POOL_WIDTH + 3 * NA_WIDTH
    for c in range(GATE_WIDTH // 512):
        sl = slice(512 * c, 512 * (c + 1))
        gate_ref[:, sl] = jax.nn.sigmoid(proj(g0 + 512 * c, 512) + b_ref[:, sl])


def _inproj(x2d, norm_g, w_in, b_gate, *, tm):
    n = x2d.shape[0]
    row = lambda w: pl.BlockSpec((tm, w), lambda i: (i, 0))
    return pl.pallas_call(
        _inproj_kernel,
        grid=(n // tm,),
        in_specs=[row(D_MODEL), _const_spec((1, D_MODEL)), _const_spec((D_MODEL, IN_WIDTH)),
                  _const_spec((1, GATE_WIDTH))],
        out_specs=[row(POOL_WIDTH), row(NA_WIDTH), row(NA_WIDTH), row(NA_WIDTH), row(GATE_WIDTH)],
        out_shape=[jax.ShapeDtypeStruct((n, POOL_WIDTH), _F32),
                   jax.ShapeDtypeStruct((n, NA_WIDTH), _BF16),
                   jax.ShapeDtypeStruct((n, NA_WIDTH), _BF16),
                   jax.ShapeDtypeStruct((n, NA_WIDTH), _BF16),
                   jax.ShapeDtypeStruct((n, GATE_WIDTH), _F32)],
        compiler_params=pltpu.CompilerParams(
            dimension_semantics=("arbitrary",), vmem_limit_bytes=VMEM_LIMIT_BYTES),
        name="inproj",
    )(x2d, norm_g, w_in, b_gate)


def _bias_tiles(rpb):
    qc = np.arange(GRID_W)[:, None]
    kc = np.arange(GRID_W)[None, :]
    cs = np.clip(qc - NA_COLS // 2, 0, GRID_W - NA_COLS)
    valid = (kc >= cs) & (kc < cs + NA_COLS)
    dc = np.clip(kc - qc + NA_COLS - 1, 0, 2 * NA_COLS - 2)
    t = jnp.where(jnp.asarray(valid)[None, None], rpb.astype(_F32)[:, :, dc], NEG_INF)
    t2 = jnp.concatenate([t[:, :-1], t[:, 1:]], axis=-1)
    return jnp.concatenate([t2[0::2], t2[1::2]], axis=2)


def _attn_kernel(q_ref, k_ref, v_ref, tb_ref, o_ref, *, rows, rb):
    blk = pl.program_id(1)
    even_head = lax.broadcasted_iota(jnp.int32, (GRID_W, LANES), 1) < NA_HEAD_DIM

    def row_body(j, carry):
        r = blk * rb + j
        rs = jnp.clip(r - NA_ROWS // 2, 0, rows - NA_ROWS)
        delta = r - rs
        q0 = pl.multiple_of(j * GRID_W, GRID_W)
        k0 = pl.multiple_of(rs * GRID_W, GRID_W)
        for hp in range(HEAD_PAIRS):
            ls = slice(hp * LANES, (hp + 1) * LANES)
            qp = q_ref[0, pl.ds(q0, GRID_W), ls]
            zero = jnp.zeros_like(qp)
            q2 = jnp.concatenate([jnp.where(even_head, qp, zero), jnp.where(even_head, zero, qp)], axis=0)
            kp = k_ref[0, pl.ds(k0, NA_KEYS), ls]
            vp = v_ref[0, pl.ds(k0, NA_KEYS), ls]
            s = lax.dot_general(q2, kp, (((1,), (1,)), ((), ())), preferred_element_type=_F32)
            bias = jnp.concatenate(
                [tb_ref[hp, 2 * m - delta + (NA_ROWS - 1)] for m in range(NA_ROWS // 2)], axis=1)
            s = s + bias
            e = jnp.exp(s - jnp.max(s, axis=-1, keepdims=True))
            l = jnp.sum(e, axis=-1, keepdims=True)
            o2 = jnp.dot(e.astype(_BF16), vp, preferred_element_type=_F32) / l
            out = jnp.where(even_head, o2[:GRID_W], o2[GRID_W:])
            o_ref[0, pl.ds(q0, GRID_W), ls] = out.astype(_BF16)
        return carry

    lax.fori_loop(0, rb, row_body, 0)


def _attention(q, k, v, bias_tiles, *, rb):
    b, s, _ = q.shape
    rows = s // GRID_W
    tq = rb * GRID_W
    qspec = pl.BlockSpec((1, tq, NA_WIDTH), lambda bi, i: (bi, i, 0))
    kvspec = pl.BlockSpec((1, s, NA_WIDTH), lambda bi, i: (bi, 0, 0))
    return pl.pallas_call(
        functools.partial(_attn_kernel, rows=rows, rb=rb),
        grid=(b, rows // rb),
        in_specs=[qspec, kvspec, kvspec, _const_spec(bias_tiles.shape)],
        out_specs=qspec,
        out_shape=jax.ShapeDtypeStruct((b, s, NA_WIDTH), _BF16),
        compiler_params=pltpu.CompilerParams(
            dimension_semantics=("arbitrary", "arbitrary"), vmem_limit_bytes=VMEM_LIMIT_BYTES),
        name="natten",
    )(q, k, v, bias_tiles)


def _mix_mlp_kernel(x_ref, p_ref, pprev_ref, pnext_ref, gate_ref, att_ref,
                    wgrp_ref, pscale_ref, wpool_ref, wna_ref, wout_ref,
                    nmlp_ref, w1_ref, w2_ref, nfin_ref, out_ref, ext_ref, *, tm, seq):
    s0 = (pl.program_id(0) * tm) % seq
    ext_ref[0:POOL_HALO, :] = jnp.where(s0 == 0, 0.0, pprev_ref[...])
    ext_ref[POOL_HALO:POOL_HALO + tm, :] = p_ref[...]
    ext_ref[POOL_HALO + tm:, :] = jnp.where(s0 + tm == seq, 0.0, pnext_ref[...])

    pos = s0 + lax.broadcasted_iota(jnp.int32, (tm, POOL_GROUP_W), 0)
    mixed = []
    for g, w in enumerate(POOL_WINDOWS):
        ls = slice(g * POOL_GROUP_W, (g + 1) * POOL_GROUP_W)
        h = w // 2
        tot = ext_ref[POOL_HALO - h:POOL_HALO - h + tm, ls]
        for d in range(-h + 1, h):
            tot = tot + ext_ref[POOL_HALO + d:POOL_HALO + d + tm, ls]
        cnt = (jnp.minimum(pos + h, seq) - jnp.maximum(pos - h, 0)).astype(_F32)
        pooled = tot / cnt - p_ref[:, ls]
        mixed.append(jnp.dot(pooled.astype(_BF16), wgrp_ref[g], preferred_element_type=_F32))
    mixed = jnp.concatenate(mixed, axis=1) * pscale_ref[...]
    pool_out = jnp.dot(mixed.astype(_BF16), wpool_ref[...], preferred_element_type=_F32)
    na_out = jnp.dot(att_ref[...], wna_ref[...], preferred_element_type=_F32)
    merged = gate_ref[:, :D_MODEL] * pool_out + gate_ref[:, D_MODEL:] * na_out
    x1 = x_ref[...] + jnp.dot(merged.astype(_BF16), wout_ref[...], preferred_element_type=_F32)

    xn = _rms(x1, nmlp_ref[...]).astype(_BF16)
    y = x1
    for c in range(D_FF // FF_CHUNK):
        cs = slice(c * FF_CHUNK, (c + 1) * FF_CHUNK)
        hid = jnp.square(jnp.maximum(jnp.dot(xn, w1_ref[:, cs], preferred_element_type=_F32), 0.0))
        y = y + jnp.dot(hid.astype(_BF16), w2_ref[cs, :], preferred_element_type=_F32)
    out_ref[...] = _rms(y, nfin_ref[...])


def _mix_mlp(x2d, p, gates, att, wgrp, pscale, wpool, wna, wout, nmlp, w1, w2, nfin, *, tm, seq):
    n = x2d.shape[0]
    hb = tm // POOL_HALO
    n_hb = n // POOL_HALO
    row = lambda w: pl.BlockSpec((tm, w), lambda i: (i, 0))
    prev = pl.BlockSpec((POOL_HALO, POOL_WIDTH), lambda i: (jnp.maximum(i * hb - 1, 0), 0))
    nxt = pl.BlockSpec((POOL_HALO, POOL_WIDTH), lambda i: (jnp.minimum((i + 1) * hb, n_hb - 1), 0))
    return pl.pallas_call(
        functools.partial(_mix_mlp_kernel, tm=tm, seq=seq),
        grid=(n // tm,),
        in_specs=[row(D_MODEL), row(POOL_WIDTH), prev, nxt, row(GATE_WIDTH), row(NA_WIDTH),
                  _const_spec(wgrp.shape), _const_spec(pscale.shape), _const_spec(wpool.shape),
                  _const_spec(wna.shape), _const_spec(wout.shape), _const_spec(nmlp.shape),
                  _const_spec(w1.shape), _const_spec(w2.shape), _const_spec(nfin.shape)],
        out_specs=row(D_MODEL),
        out_shape=jax.ShapeDtypeStruct((n, D_MODEL), _F32),
        scratch_shapes=[pltpu.VMEM((tm + 2 * POOL_HALO, POOL_WIDTH), _F32)],
        compiler_params=pltpu.CompilerParams(
            dimension_semantics=("arbitrary",), vmem_limit_bytes=VMEM_LIMIT_BYTES),
        name="mix_mlp",
    )(x2d, p, p, p, gates, att, wgrp, pscale, wpool, wna, wout, nmlp, w1, w2, nfin)


def _layer(x, wts, bias_tiles, *, tm_in, tm_mlp, rb):
    b, s, d = x.shape
    x2d = x.reshape(b * s, d)
    p, q, k, v, gates = _inproj(x2d, wts["norm_mix"], wts["w_in"], wts["b_gate"], tm=tm_in)
    shp = (b, s, NA_WIDTH)
    att = _attention(q.reshape(shp), k.reshape(shp), v.reshape(shp), bias_tiles, rb=rb)
    y = _mix_mlp(x2d, p, gates, att.reshape(b * s, NA_WIDTH), wts["w_pool_grp"], wts["pool_scale"],
                 wts["w_pool_proj"], wts["w_na_proj"], wts["w_out"], wts["norm_mlp"],
                 wts["w_ff1"], wts["w_ff2"], wts["norm_final"], tm=tm_mlp, seq=s)
    return y.reshape(b, s, d)


def kernel(x_prompt, x_sample, norm_mix, w_in, b_gate, w_pool_grp, pool_scale, w_pool_proj, rpb,
           w_na_proj, w_out, norm_mlp, w_ff1, w_ff2, norm_final):
    depth = norm_mix.shape[0]
    assert depth == 1
    l = 0
    wts = {
        "norm_mix": norm_mix[l].reshape(1, D_MODEL),
        "w_in": w_in[l].astype(_BF16),
        "b_gate": b_gate[l].reshape(1, GATE_WIDTH),
        "w_pool_grp": w_pool_grp[l].astype(_BF16),
        "pool_scale": pool_scale[l].reshape(1, POOL_WIDTH),
        "w_pool_proj": w_pool_proj[l].astype(_BF16),
        "w_na_proj": w_na_proj[l].astype(_BF16),
        "w_out": w_out[l].astype(_BF16),
        "norm_mlp": norm_mlp[l].reshape(1, D_MODEL),
        "w_ff1": w_ff1[l].astype(_BF16),
        "w_ff2": w_ff2[l].astype(_BF16),
        "norm_final": norm_final.reshape(1, D_MODEL),
    }
    bias_tiles = _bias_tiles(rpb[l])
    run = functools.partial(_layer, wts=wts, bias_tiles=bias_tiles, tm_in=512, tm_mlp=256, rb=8)
    return (run(x_prompt), run(x_sample))
```

```python
import functools

import numpy as np
import jax
import jax.numpy as jnp
from jax import lax
from jax.experimental import pallas as pl
from jax.experimental.pallas import tpu as pltpu

D_MODEL = 1024
GRID_W = 64
POOL_WIDTH = 512
POOL_GROUPS = 4
POOL_GROUP_W = POOL_WIDTH // POOL_GROUPS
POOL_WINDOWS = (2, 4, 8, 16)
POOL_HALO = max(POOL_WINDOWS) // 2
NA_HEADS = 8
NA_HEAD_DIM = 64
NA_WIDTH = NA_HEADS * NA_HEAD_DIM
NA_ROWS = 8
NA_COLS = 16
NA_KEYS = NA_ROWS * GRID_W
HEAD_PAIRS = NA_HEADS // 2
ALL_HEAD_PAIRS = tuple(range(HEAD_PAIRS))
N_BIAS_TILES = 2 * NA_ROWS - 2
D_FF = 4 * D_MODEL
FF_CHUNK = 1024
GATE_WIDTH = 2 * D_MODEL
IN_WIDTH = POOL_WIDTH + 3 * NA_WIDTH + GATE_WIDTH
RMS_EPS = 1e-6
NEG_INF = -1e30
LOG2_E = 1.4426950408889634
QK_SCALE = NA_HEAD_DIM ** -0.5 * LOG2_E
SOFTMAX_ROWS = 16

LANES = 128
VMEM_LIMIT_BYTES = 56 * 1024 * 1024

_BF16 = jnp.bfloat16
_F32 = jnp.float32


def _rms(x, g):
    var = jnp.mean(x * x, axis=-1, keepdims=True)
    return x * lax.rsqrt(var + RMS_EPS) * g


def _const_spec(shape):
    zeros = (0,) * len(shape)
    return pl.BlockSpec(shape, lambda *_: zeros, pipeline_mode=pl.Buffered(1))


def _inproj_kernel(x_ref, g_ref, w_ref, b_ref, p_ref, q_ref, k_ref, v_ref, gate_ref):
    xn = _rms(x_ref[...], g_ref[...]).astype(_BF16)

    def proj(lo, width):
        return jnp.dot(xn, w_ref[:, lo:lo + width], preferred_element_type=_F32)

    p_ref[...] = proj(0, POOL_WIDTH)
    q_ref[...] = (proj(POOL_WIDTH, NA_WIDTH) * QK_SCALE).astype(_BF16)
    k_ref[...] = proj(POOL_WIDTH + NA_WIDTH, NA_WIDTH).astype(_BF16)
    v_ref[...] = proj(POOL_WIDTH + 2 * NA_WIDTH, NA_WIDTH).astype(_BF16)
    g0 = POOL_WIDTH + 3 * NA_WIDTH
    for c in range(GATE_WIDTH // 512):
        sl = slice(512 * c, 512 * (c + 1))
        gate_ref[:, sl] = jax.nn.sigmoid(proj(g0 + 512 * c, 512) + b_ref[:, sl])


def _inproj(x2d, norm_g, w_in, b_gate, *, tm):
    n = x2d.shape[0]
    row = lambda w: pl.BlockSpec((tm, w), lambda i: (i, 0))
    return pl.pallas_call(
        _inproj_kernel,
        grid=(n // tm,),
        in_specs=[row(D_MODEL), _const_spec((1, D_MODEL)), _const_spec((D_MODEL, IN_WIDTH)),
                  _const_spec((1, GATE_WIDTH))],
        out_specs=[row(POOL_WIDTH), row(NA_WIDTH), row(NA_WIDTH), row(NA_WIDTH), row(GATE_WIDTH)],
        out_shape=[jax.ShapeDtypeStruct((n, POOL_WIDTH), _F32),
                   jax.ShapeDtypeStruct((n, NA_WIDTH), _BF16),
                   jax.ShapeDtypeStruct((n, NA_WIDTH), _BF16),
                   jax.ShapeDtypeStruct((n, NA_WIDTH), _BF16),
                   jax.ShapeDtypeStruct((n, GATE_WIDTH), _F32)],
        compiler_params=pltpu.CompilerParams(
            dimension_semantics=("arbitrary",), vmem_limit_bytes=VMEM_LIMIT_BYTES),
        name="inproj",
    )(x2d, norm_g, w_in, b_gate)


def _bias_tiles(rpb):
    qc = np.arange(GRID_W)[:, None]
    kc = np.arange(GRID_W)[None, :]
    cs = np.clip(qc - NA_COLS // 2, 0, GRID_W - NA_COLS)
    valid = (kc >= cs) & (kc < cs + NA_COLS)
    dc = np.clip(kc - qc + NA_COLS - 1, 0, 2 * NA_COLS - 2)
    t = jnp.where(jnp.asarray(valid)[None, None], rpb.astype(_F32)[:, :, dc] * LOG2_E, NEG_INF)
    t2 = jnp.concatenate([t[:, :-1], t[:, 1:]], axis=-1)
    return jnp.concatenate([t2[0::2], t2[1::2]], axis=2)


def _attn_kernel(q_ref, k_ref, v_ref, tb_ref, o_ref, s0_ref, s1_ref, p0_ref, p1_ref, m0_ref, m1_ref,
                 vones_ref, *, rows, rb, unroll, span):
    blk = pl.program_id(1)
    even_head = lax.broadcasted_iota(jnp.int32, (GRID_W, LANES), 1) < NA_HEAD_DIM
    s_slots, p_slots, m_slots = (s0_ref, s1_ref), (p0_ref, p1_ref), (m0_ref, m1_ref)
    lane_tiles = [slice(m * LANES, (m + 1) * LANES) for m in range(NA_KEYS // LANES)]
    base_row = jnp.clip(blk * rb - NA_ROWS // 2, 0, rows - span)

    @pl.when((pl.program_id(0) == 0) & (blk == 0))
    def _():
        vones_ref[:, :, LANES:] = jnp.ones((HEAD_PAIRS, span * GRID_W, LANES), _BF16)

    v0 = pl.multiple_of(base_row * GRID_W, GRID_W)
    for hp in ALL_HEAD_PAIRS:
        vones_ref[hp, :, :LANES] = v_ref[0, pl.ds(v0, span * GRID_W), hp * LANES:(hp + 1) * LANES]

    def key_rows(j):
        r = blk * rb + j
        rs = jnp.clip(r - NA_ROWS // 2, 0, rows - NA_ROWS)
        return rs, r - rs

    def scores(j, slot, hps=ALL_HEAD_PAIRS):
        rs, delta = key_rows(j)
        k0 = pl.multiple_of(rs * GRID_W, GRID_W)
        q0 = pl.multiple_of(j * GRID_W, GRID_W)
        for hp in hps:
            ls = slice(hp * LANES, (hp + 1) * LANES)
            qp = q_ref[0, pl.ds(q0, GRID_W), ls]
            zero = jnp.zeros_like(qp)
            q2 = jnp.concatenate([jnp.where(even_head, qp, zero), jnp.where(even_head, zero, qp)], axis=0)
            kp = k_ref[0, pl.ds(k0, NA_KEYS), ls]
            qk = lax.dot_general(q2, kp, (((1,), (1,)), ((), ())), preferred_element_type=_F32)
            s = [qk[:, lt] + tb_ref[hp, 2 * m - delta + (NA_ROWS - 1)] for m, lt in enumerate(lane_tiles)]
            for sm, lt in zip(s, lane_tiles):
                s_slots[slot][hp, :, lt] = sm
            mx = jnp.max(functools.reduce(jnp.maximum, s), axis=-1, keepdims=True)
            m_slots[slot][hp] = jnp.broadcast_to(mx, (2 * GRID_W, LANES))

    def softmax(j, slot, hps=ALL_HEAD_PAIRS):
        for hp in hps:
            mx = m_slots[slot][hp]
            for lt in lane_tiles:
                p_slots[slot][hp, :, lt] = jnp.exp2(s_slots[slot][hp, :, lt] - mx).astype(_BF16)

    def weighted_values(j, slot, hps=ALL_HEAD_PAIRS):
        rs, _ = key_rows(j)
        k0 = pl.multiple_of((rs - base_row) * GRID_W, GRID_W)
        q0 = pl.multiple_of(j * GRID_W, GRID_W)
        for hp in hps:
            ls = slice(hp * LANES, (hp + 1) * LANES)
            vo = vones_ref[hp, pl.ds(k0, NA_KEYS), :]
            o2l = jnp.dot(p_slots[slot][hp], vo, preferred_element_type=_F32)
            o2 = o2l[:, :LANES] / o2l[:, LANES:]
            out = jnp.where(even_head, o2[:GRID_W], o2[GRID_W:])
            o_ref[0, pl.ds(q0, GRID_W), ls] = out.astype(_BF16)

    scores(0, 0)
    scores(1, 1)
    softmax(0, 0)

    def pipeline_steps(u, carry):
        for i in range(unroll):
            t = 1 + unroll * u + i
            scores(t + 1, i % 2)
            softmax(t, (1 + i) % 2)
            weighted_values(t - 1, i % 2)
        return carry

    trips = (rb - 2) // unroll
    if trips == 1:
        pipeline_steps(0, 0)
    else:
        lax.fori_loop(0, trips, pipeline_steps, 0)
    softmax(rb - 1, 1)
    weighted_values(rb - 2, 0)
    weighted_values(rb - 1, 1)


def _attention(q, k, v, bias_tiles, *, rb, unroll):
    b, s, _ = q.shape
    rows = s // GRID_W
    assert unroll % 2 == 0 and (rb - 2) % unroll == 0 and rows % rb == 0
    tq = rb * GRID_W
    qspec = pl.BlockSpec((1, tq, NA_WIDTH), lambda bi, i: (bi, i, 0))
    kvspec = pl.BlockSpec((1, s, NA_WIDTH), lambda bi, i: (bi, 0, 0))
    pair_rows = 2 * GRID_W
    span = min(rb + NA_ROWS, rows)
    return pl.pallas_call(
        functools.partial(_attn_kernel, rows=rows, rb=rb, unroll=unroll, span=span),
        grid=(b, rows // rb),
        in_specs=[qspec, kvspec, kvspec, _const_spec(bias_tiles.shape)],
        out_specs=qspec,
        out_shape=jax.ShapeDtypeStruct((b, s, NA_WIDTH), _BF16),
        scratch_shapes=[pltpu.VMEM((HEAD_PAIRS, pair_rows, NA_KEYS), _F32)] * 2
        + [pltpu.VMEM((HEAD_PAIRS, pair_rows, NA_KEYS), _BF16)] * 2
        + [pltpu.VMEM((HEAD_PAIRS, pair_rows, LANES), _F32)] * 2
        + [pltpu.VMEM((HEAD_PAIRS, span * GRID_W, 2 * LANES), _BF16)],
        compiler_params=pltpu.CompilerParams(
            dimension_semantics=("arbitrary", "arbitrary"), vmem_limit_bytes=VMEM_LIMIT_BYTES),
        name="natten",
    )(q, k, v, bias_tiles)


def _mix_mlp_kernel(x_ref, p_ref, pprev_ref, pnext_ref, gate_ref, att_ref,
                    wgrp_ref, pscale_ref, wpool_ref, wna_ref, wout_ref,
                    nmlp_ref, w1_ref, w2_ref, nfin_ref, out_ref, ext_ref, *, tm, seq):
    s0 = (pl.program_id(0) * tm) % seq
    ext_ref[0:POOL_HALO, :] = jnp.where(s0 == 0, 0.0, pprev_ref[...])
    ext_ref[POOL_HALO:POOL_HALO + tm, :] = p_ref[...]
    ext_ref[POOL_HALO + tm:, :] = jnp.where(s0 + tm == seq, 0.0, pnext_ref[...])

    pos = s0 + lax.broadcasted_iota(jnp.int32, (tm, POOL_GROUP_W), 0)
    mixed = []
    for g, w in enumerate(POOL_WINDOWS):
        ls = slice(g * POOL_GROUP_W, (g + 1) * POOL_GROUP_W)
        h = w // 2
        tot = ext_ref[POOL_HALO - h:POOL_HALO - h + tm, ls]
        for d in range(-h + 1, h):
            tot = tot + ext_ref[POOL_HALO + d:POOL_HALO + d + tm, ls]
        cnt = (jnp.minimum(pos + h, seq) - jnp.maximum(pos - h, 0)).astype(_F32)
        pooled = tot / cnt - p_ref[:, ls]
        mixed.append(jnp.dot(pooled.astype(_BF16), wgrp_ref[g], preferred_element_type=_F32))
    mixed = jnp.concatenate(mixed, axis=1) * pscale_ref[...]
    pool_out = jnp.dot(mixed.astype(_BF16), wpool_ref[...], preferred_element_type=_F32)
    na_out = jnp.dot(att_ref[...], wna_ref[...], preferred_element_type=_F32)
    merged = gate_ref[:, :D_MODEL] * pool_out + gate_ref[:, D_MODEL:] * na_out
    x1 = x_ref[...] + jnp.dot(merged.astype(_BF16), wout_ref[...], preferred_element_type=_F32)

    xn = _rms(x1, nmlp_ref[...]).astype(_BF16)
    y = x1
    for c in range(D_FF // FF_CHUNK):
        cs = slice(c * FF_CHUNK, (c + 1) * FF_CHUNK)
        hid = jnp.square(jnp.maximum(jnp.dot(xn, w1_ref[:, cs], preferred_element_type=_F32), 0.0))
        y = y + jnp.dot(hid.astype(_BF16), w2_ref[cs, :], preferred_element_type=_F32)
    out_ref[...] = _rms(y, nfin_ref[...])


def _mix_mlp(x2d, p, gates, att, wgrp, pscale, wpool, wna, wout, nmlp, w1, w2, nfin, *, tm, seq):
    n = x2d.shape[0]
    hb = tm // POOL_HALO
    n_hb = n // POOL_HALO
    row = lambda w: pl.BlockSpec((tm, w), lambda i: (i, 0))
    prev = pl.BlockSpec((POOL_HALO, POOL_WIDTH), lambda i: (jnp.maximum(i * hb - 1, 0), 0))
    nxt = pl.BlockSpec((POOL_HALO, POOL_WIDTH), lambda i: (jnp.minimum((i + 1) * hb, n_hb - 1), 0))
    return pl.pallas_call(
        functools.partial(_mix_mlp_kernel, tm=tm, seq=seq),
        grid=(n // tm,),
        in_specs=[row(D_MODEL), row(POOL_WIDTH), prev, nxt, row(GATE_WIDTH), row(NA_WIDTH),
                  _const_spec(wgrp.shape), _const_spec(pscale.shape), _const_spec(wpool.shape),
                  _const_spec(wna.shape), _const_spec(wout.shape), _const_spec(nmlp.shape),
                  _const_spec(w1.shape), _const_spec(w2.shape), _const_spec(nfin.shape)],
        out_specs=row(D_MODEL),
        out_shape=jax.ShapeDtypeStruct((n, D_MODEL), _F32),
        scratch_shapes=[pltpu.VMEM((tm + 2 * POOL_HALO, POOL_WIDTH), _F32)],
        compiler_params=pltpu.CompilerParams(
            dimension_semantics=("arbitrary",), vmem_limit_bytes=VMEM_LIMIT_BYTES),
        name="mix_mlp",
    )(x2d, p, p, p, gates, att, wgrp, pscale, wpool, wna, wout, nmlp, w1, w2, nfin)


def _layer(x, wts, bias_tiles, *, tm_in, tm_mlp, rb, unroll):
    b, s, d = x.shape
    x2d = x.reshape(b * s, d)
    p, q, k, v, gates = _inproj(x2d, wts["norm_mix"], wts["w_in"], wts["b_gate"], tm=tm_in)
    shp = (b, s, NA_WIDTH)
    att = _attention(q.reshape(shp), k.reshape(shp), v.reshape(shp), bias_tiles, rb=rb, unroll=unroll)
    y = _mix_mlp(x2d, p, gates, att.reshape(b * s, NA_WIDTH), wts["w_pool_grp"], wts["pool_scale"],
                 wts["w_pool_proj"], wts["w_na_proj"], wts["w_out"], wts["norm_mlp"],
                 wts["w_ff1"], wts["w_ff2"], wts["norm_final"], tm=tm_mlp, seq=s)
    return y.reshape(b, s, d)


def kernel(x_prompt, x_sample, norm_mix, w_in, b_gate, w_pool_grp, pool_scale, w_pool_proj, rpb,
           w_na_proj, w_out, norm_mlp, w_ff1, w_ff2, norm_final):
    depth = norm_mix.shape[0]
    assert depth == 1
    l = 0
    wts = {
        "norm_mix": norm_mix[l].reshape(1, D_MODEL),
        "w_in": w_in[l].astype(_BF16),
        "b_gate": b_gate[l].reshape(1, GATE_WIDTH),
        "w_pool_grp": w_pool_grp[l].astype(_BF16),
        "pool_scale": pool_scale[l].reshape(1, POOL_WIDTH),
        "w_pool_proj": w_pool_proj[l].astype(_BF16),
        "w_na_proj": w_na_proj[l].astype(_BF16),
        "w_out": w_out[l].astype(_BF16),
        "norm_mlp": norm_mlp[l].reshape(1, D_MODEL),
        "w_ff1": w_ff1[l].astype(_BF16),
        "w_ff2": w_ff2[l].astype(_BF16),
        "norm_final": norm_final.reshape(1, D_MODEL),
    }
    bias_tiles = _bias_tiles(rpb[l])
    run = functools.partial(_layer, wts=wts, bias_tiles=bias_tiles, tm_in=512, tm_mlp=256, rb=16, unroll=2)
    return (run(x_prompt), run(x_sample))
```

```python
import functools

import jax
import jax.numpy as jnp
from jax import lax
from jax.experimental import pallas as pl
from jax.experimental.pallas import tpu as pltpu

D_MODEL = 1024
GRID_W = 64
POOL_WIDTH = 512
POOL_GROUPS = 4
POOL_GROUP_W = POOL_WIDTH // POOL_GROUPS
POOL_WINDOWS = (2, 4, 8, 16)
POOL_HALO = max(POOL_WINDOWS) // 2
NA_HEADS = 8
NA_HEAD_DIM = 64
NA_WIDTH = NA_HEADS * NA_HEAD_DIM
NA_ROWS = 8
NA_COLS = 16
NA_KEYS = NA_ROWS * GRID_W
HEAD_PAIRS = NA_HEADS // 2
ALL_HEAD_PAIRS = tuple(range(HEAD_PAIRS))
N_ROW_OFFSETS = 2 * NA_ROWS - 1
N_BIAS_TILES = N_ROW_OFFSETS - 1
D_FF = 4 * D_MODEL
FF_CHUNK = 1024
MLP_SUB_TILE = 512
GATE_WIDTH = 2 * D_MODEL
IN_WIDTH = POOL_WIDTH + 3 * NA_WIDTH + GATE_WIDTH
RMS_EPS = 1e-6
NEG_INF = -1e30
LOG2_E = 1.4426950408889634
QK_SCALE = NA_HEAD_DIM ** -0.5 * LOG2_E
SOFTMAX_ROWS = 16

LANES = 128
VMEM_LIMIT_BYTES = 56 * 1024 * 1024

_BF16 = jnp.bfloat16
_F32 = jnp.float32


def _rms(x, g):
    var = jnp.mean(x * x, axis=-1, keepdims=True)
    return x * lax.rsqrt(var + RMS_EPS) * g


def _const_spec(shape):
    zeros = (0,) * len(shape)
    return pl.BlockSpec(shape, lambda *_: zeros, pipeline_mode=pl.Buffered(1))


def _inproj_kernel(x_ref, g_ref, w_ref, b_ref, p_ref, q_ref, k_ref, v_ref, gate_ref):
    xn = _rms(x_ref[...], g_ref[...]).astype(_BF16)

    def proj(lo, width):
        return jnp.dot(xn, w_ref[:, lo:lo + width], preferred_element_type=_F32)

    p_ref[...] = proj(0, POOL_WIDTH)
    q_ref[...] = (proj(POOL_WIDTH, NA_WIDTH) * QK_SCALE).astype(_BF16)
    k_ref[...] = proj(POOL_WIDTH + NA_WIDTH, NA_WIDTH).astype(_BF16)
    v_ref[...] = proj(POOL_WIDTH + 2 * NA_WIDTH, NA_WIDTH).astype(_BF16)
    g0 = POOL_WIDTH + 3 * NA_WIDTH
    for c in range(GATE_WIDTH // 512):
        sl = slice(512 * c, 512 * (c + 1))
        gate_ref[:, sl] = jax.nn.sigmoid(proj(g0 + 512 * c, 512) + b_ref[:, sl])


def _inproj(x2d, norm_g, w_in, b_gate, *, tm):
    n = x2d.shape[0]
    row = lambda w: pl.BlockSpec((tm, w), lambda i: (i, 0))
    return pl.pallas_call(
        _inproj_kernel,
        grid=(n // tm,),
        in_specs=[row(D_MODEL), _const_spec((1, D_MODEL)), _const_spec((D_MODEL, IN_WIDTH)),
                  _const_spec((1, GATE_WIDTH))],
        out_specs=[row(POOL_WIDTH), row(NA_WIDTH), row(NA_WIDTH), row(NA_WIDTH), row(GATE_WIDTH)],
        out_shape=[jax.ShapeDtypeStruct((n, POOL_WIDTH), _F32),
                   jax.ShapeDtypeStruct((n, NA_WIDTH), _BF16),
                   jax.ShapeDtypeStruct((n, NA_WIDTH), _BF16),
                   jax.ShapeDtypeStruct((n, NA_WIDTH), _BF16),
                   jax.ShapeDtypeStruct((n, GATE_WIDTH), _F32)],
        compiler_params=pltpu.CompilerParams(
            dimension_semantics=("arbitrary",), vmem_limit_bytes=VMEM_LIMIT_BYTES),
        name="inproj",
    )(x2d, norm_g, w_in, b_gate)


def _bias_rows(rpb):
    lo = GRID_W - NA_COLS
    ext = jnp.pad(rpb.astype(_F32) * LOG2_E, ((0, 0), (0, 0), (lo, LANES - lo - (2 * NA_COLS - 1))), mode="edge")
    return ext.reshape(NA_HEADS * N_ROW_OFFSETS, LANES)


def _build_bias_tiles(rext_ref, tb_ref):
    qc = lax.broadcasted_iota(jnp.int32, (GRID_W, LANES), 0)
    lane = lax.broadcasted_iota(jnp.int32, (GRID_W, LANES), 1)
    kc = lane & (GRID_W - 1)
    cs = jnp.clip(qc - NA_COLS // 2, 0, GRID_W - NA_COLS)
    valid = (kc >= cs) & (kc < cs + NA_COLS)
    left = lane < GRID_W
    for h in range(NA_HEADS):
        rows = slice((h % 2) * GRID_W, (h % 2 + 1) * GRID_W)

        def shifted(j, first_lane):
            row = jnp.broadcast_to(rext_ref[pl.ds(h * N_ROW_OFFSETS + j, 1), :], (GRID_W, LANES))
            return pltpu.roll(row, (first_lane + GRID_W + 1) % LANES, 1, stride=1, stride_axis=0)

        for j in range(N_BIAS_TILES):
            tile = jnp.where(left, shifted(j, 0), shifted(j + 1, GRID_W))
            tb_ref[h // 2, j, rows, :] = jnp.where(valid, tile, NEG_INF)


def _attn_kernel(q_ref, k_ref, v_ref, rext_ref, o_ref, s0_ref, s1_ref, p0_ref, p1_ref, m0_ref, m1_ref,
                 vones_ref, tb_ref, *, rows, rb, unroll, span):
    blk = pl.program_id(1)
    even_head = lax.broadcasted_iota(jnp.int32, (GRID_W, LANES), 1) < NA_HEAD_DIM
    s_slots, p_slots, m_slots = (s0_ref, s1_ref), (p0_ref, p1_ref), (m0_ref, m1_ref)
    lane_tiles = [slice(m * LANES, (m + 1) * LANES) for m in range(NA_KEYS // LANES)]
    base_row = jnp.clip(blk * rb - NA_ROWS // 2, 0, rows - span)

    @pl.when((pl.program_id(0) == 0) & (blk == 0))
    def _():
        vones_ref[:, :, LANES:] = jnp.ones((HEAD_PAIRS, span * GRID_W, LANES), _BF16)
        _build_bias_tiles(rext_ref, tb_ref)

    v0 = pl.multiple_of(base_row * GRID_W, GRID_W)
    for hp in ALL_HEAD_PAIRS:
        vones_ref[hp, :, :LANES] = v_ref[0, pl.ds(v0, span * GRID_W), hp * LANES:(hp + 1) * LANES]

    def key_rows(j):
        r = blk * rb + j
        rs = jnp.clip(r - NA_ROWS // 2, 0, rows - NA_ROWS)
        return rs, r - rs

    def scores(j, slot, hps=ALL_HEAD_PAIRS):
        rs, delta = key_rows(j)
        k0 = pl.multiple_of(rs * GRID_W, GRID_W)
        q0 = pl.multiple_of(j * GRID_W, GRID_W)
        for hp in hps:
            ls = slice(hp * LANES, (hp + 1) * LANES)
            qp = q_ref[0, pl.ds(q0, GRID_W), ls]
            zero = jnp.zeros_like(qp)
            q2 = jnp.concatenate([jnp.where(even_head, qp, zero), jnp.where(even_head, zero, qp)], axis=0)
            kp = k_ref[0, pl.ds(k0, NA_KEYS), ls]
            qk = lax.dot_general(q2, kp, (((1,), (1,)), ((), ())), preferred_element_type=_F32)
            s = [qk[:, lt] + tb_ref[hp, 2 * m - delta + (NA_ROWS - 1)] for m, lt in enumerate(lane_tiles)]
            for sm, lt in zip(s, lane_tiles):
                s_slots[slot][hp, :, lt] = sm
            mx = jnp.max(functools.reduce(jnp.maximum, s), axis=-1, keepdims=True)
            m_slots[slot][hp] = jnp.broadcast_to(mx, (2 * GRID_W, LANES))

    def softmax(j, slot, hps=ALL_HEAD_PAIRS):
        for hp in hps:
            mx = m_slots[slot][hp]
            for lt in lane_tiles:
                p_slots[slot][hp, :, lt] = jnp.exp2(s_slots[slot][hp, :, lt] - mx).astype(_BF16)

    def weighted_values(j, slot, hps=ALL_HEAD_PAIRS):
        rs, _ = key_rows(j)
        k0 = pl.multiple_of((rs - base_row) * GRID_W, GRID_W)
        q0 = pl.multiple_of(j * GRID_W, GRID_W)
        for hp in hps:
            ls = slice(hp * LANES, (hp + 1) * LANES)
            vo = vones_ref[hp, pl.ds(k0, NA_KEYS), :]
            o2l = jnp.dot(p_slots[slot][hp], vo, preferred_element_type=_F32)
            o2 = o2l[:, :LANES] / o2l[:, LANES:]
            out = jnp.where(even_head, o2[:GRID_W], o2[GRID_W:])
            o_ref[0, pl.ds(q0, GRID_W), ls] = out.astype(_BF16)

    scores(0, 0)
    scores(1, 1)
    softmax(0, 0)

    def pipeline_steps(u, carry):
        for i in range(unroll):
            t = 1 + unroll * u + i
            scores(t + 1, i % 2)
            softmax(t, (1 + i) % 2)
            weighted_values(t - 1, i % 2)
        return carry

    trips = (rb - 2) // unroll
    if trips == 1:
        pipeline_steps(0, 0)
    else:
        lax.fori_loop(0, trips, pipeline_steps, 0)
    softmax(rb - 1, 1)
    weighted_values(rb - 2, 0)
    weighted_values(rb - 1, 1)


def _attention(q, k, v, bias_rows, *, rb, unroll):
    b, s, _ = q.shape
    rows = s // GRID_W
    assert unroll % 2 == 0 and (rb - 2) % unroll == 0 and rows % rb == 0
    tq = rb * GRID_W
    qspec = pl.BlockSpec((1, tq, NA_WIDTH), lambda bi, i: (bi, i, 0))
    kvspec = pl.BlockSpec((1, s, NA_WIDTH), lambda bi, i: (bi, 0, 0))
    pair_rows = 2 * GRID_W
    span = min(rb + NA_ROWS, rows)
    return pl.pallas_call(
        functools.partial(_attn_kernel, rows=rows, rb=rb, unroll=unroll, span=span),
        grid=(b, rows // rb),
        in_specs=[qspec, kvspec, kvspec, _const_spec(bias_rows.shape)],
        out_specs=qspec,
        out_shape=jax.ShapeDtypeStruct((b, s, NA_WIDTH), _BF16),
        scratch_shapes=[pltpu.VMEM((HEAD_PAIRS, pair_rows, NA_KEYS), _F32)] * 2
        + [pltpu.VMEM((HEAD_PAIRS, pair_rows, NA_KEYS), _BF16)] * 2
        + [pltpu.VMEM((HEAD_PAIRS, pair_rows, LANES), _F32)] * 2
        + [pltpu.VMEM((HEAD_PAIRS, span * GRID_W, 2 * LANES), _BF16),
           pltpu.VMEM((HEAD_PAIRS, N_BIAS_TILES, pair_rows, LANES), _F32)],
        compiler_params=pltpu.CompilerParams(
            dimension_semantics=("arbitrary", "arbitrary"), vmem_limit_bytes=VMEM_LIMIT_BYTES),
        name="natten",
    )(q, k, v, bias_rows)


def _mix_mlp_kernel(x_ref, p_ref, pprev_ref, pnext_ref, gate_ref, att_ref,
                    wgrp_ref, pscale_ref, wpool_ref, wna_ref, wout_ref,
                    nmlp_ref, w1_ref, w2_ref, nfin_ref, out_ref, ext_ref, *, tm, sub, seq):
    s0 = (pl.program_id(0) * tm) % seq
    ext_ref[0:POOL_HALO, :] = jnp.where(s0 == 0, 0.0, pprev_ref[...])
    ext_ref[POOL_HALO:POOL_HALO + tm, :] = p_ref[...]
    ext_ref[POOL_HALO + tm:, :] = jnp.where(s0 + tm == seq, 0.0, pnext_ref[...])

    for r0 in range(0, tm, sub):
        rows = slice(r0, r0 + sub)
        pos = (s0 + r0).astype(_F32) + lax.broadcasted_iota(jnp.int32, (sub, POOL_GROUP_W), 0).astype(_F32)
        mixed = []
        for g, w in enumerate(POOL_WINDOWS):
            ls = slice(g * POOL_GROUP_W, (g + 1) * POOL_GROUP_W)
            h = w // 2
            run = ext_ref[POOL_HALO + r0 - h:POOL_HALO + r0 + sub + h, ls]
            width = 1
            while 2 * width < w:
                n = run.shape[0] - width
                run = run[:n] + run[width:]
                width *= 2
            tot = run[:sub] + run[width:width + sub]
            cnt = jnp.minimum(pos + float(h), float(seq)) - jnp.maximum(pos - float(h), 0.0)
            pooled = tot / cnt - p_ref[rows, ls]
            mixed.append(jnp.dot(pooled.astype(_BF16), wgrp_ref[g], preferred_element_type=_F32))
        mixed = jnp.concatenate(mixed, axis=1) * pscale_ref[...]
        pool_out = jnp.dot(mixed.astype(_BF16), wpool_ref[...], preferred_element_type=_F32)
        na_out = jnp.dot(att_ref[rows, :], wna_ref[...], preferred_element_type=_F32)
        merged = gate_ref[rows, :D_MODEL] * pool_out + gate_ref[rows, D_MODEL:] * na_out
        x1 = x_ref[rows, :] + jnp.dot(merged.astype(_BF16), wout_ref[...], preferred_element_type=_F32)

        xn = _rms(x1, nmlp_ref[...]).astype(_BF16)
        y = x1
        for c in range(D_FF // FF_CHUNK):
            cs = slice(c * FF_CHUNK, (c + 1) * FF_CHUNK)
            hid = jnp.square(jnp.maximum(jnp.dot(xn, w1_ref[:, cs], preferred_element_type=_F32), 0.0))
            y = y + jnp.dot(hid.astype(_BF16), w2_ref[cs, :], preferred_element_type=_F32)
        out_ref[rows, :] = _rms(y, nfin_ref[...])


def _mix_mlp(x2d, p, gates, att, wgrp, pscale, wpool, wna, wout, nmlp, w1, w2, nfin, *, tm, seq):
    n = x2d.shape[0]
    hb = tm // POOL_HALO
    n_hb = n // POOL_HALO
    row = lambda w: pl.BlockSpec((tm, w), lambda i: (i, 0))
    prev = pl.BlockSpec((POOL_HALO, POOL_WIDTH), lambda i: (jnp.maximum(i * hb - 1, 0), 0))
    nxt = pl.BlockSpec((POOL_HALO, POOL_WIDTH), lambda i: (jnp.minimum((i + 1) * hb, n_hb - 1), 0))
    return pl.pallas_call(
        functools.partial(_mix_mlp_kernel, tm=tm, sub=MLP_SUB_TILE, seq=seq),
        grid=(n // tm,),
        in_specs=[row(D_MODEL), row(POOL_WIDTH), prev, nxt, row(GATE_WIDTH), row(NA_WIDTH),
                  _const_spec(wgrp.shape), _const_spec(pscale.shape), _const_spec(wpool.shape),
                  _const_spec(wna.shape), _const_spec(wout.shape), _const_spec(nmlp.shape),
                  _const_spec(w1.shape), _const_spec(w2.shape), _const_spec(nfin.shape)],
        out_specs=row(D_MODEL),
        out_shape=jax.ShapeDtypeStruct((n, D_MODEL), _F32),
        scratch_shapes=[pltpu.VMEM((tm + 2 * POOL_HALO, POOL_WIDTH), _F32)],
        compiler_params=pltpu.CompilerParams(
            dimension_semantics=("arbitrary",), vmem_limit_bytes=VMEM_LIMIT_BYTES),
        name="mix_mlp",
    )(x2d, p, p, p, gates, att, wgrp, pscale, wpool, wna, wout, nmlp, w1, w2, nfin)


def _layer(x, wts, bias_rows, *, tm_in, tm_mlp, rb, unroll):
    b, s, d = x.shape
    x2d = x.reshape(b * s, d)
    p, q, k, v, gates = _inproj(x2d, wts["norm_mix"], wts["w_in"], wts["b_gate"], tm=tm_in)
    shp = (b, s, NA_WIDTH)
    att = _attention(q.reshape(shp), k.reshape(shp), v.reshape(shp), bias_rows, rb=rb, unroll=unroll)
    y = _mix_mlp(x2d, p, gates, att.reshape(b * s, NA_WIDTH), wts["w_pool_grp"], wts["pool_scale"],
                 wts["w_pool_proj"], wts["w_na_proj"], wts["w_out"], wts["norm_mlp"],
                 wts["w_ff1"], wts["w_ff2"], wts["norm_final"], tm=tm_mlp, seq=s)
    return y.reshape(b, s, d)


def kernel(x_prompt, x_sample, norm_mix, w_in, b_gate, w_pool_grp, pool_scale, w_pool_proj, rpb,
           w_na_proj, w_out, norm_mlp, w_ff1, w_ff2, norm_final):
    depth = norm_mix.shape[0]
    assert depth == 1
    l = 0
    wts = {
        "norm_mix": norm_mix[l].reshape(1, D_MODEL),
        "w_in": w_in[l].astype(_BF16),
        "b_gate": b_gate[l].reshape(1, GATE_WIDTH),
        "w_pool_grp": w_pool_grp[l].astype(_BF16),
        "pool_scale": pool_scale[l].reshape(1, POOL_WIDTH),
        "w_pool_proj": w_pool_proj[l].astype(_BF16),
        "w_na_proj": w_na_proj[l].astype(_BF16),
        "w_out": w_out[l].astype(_BF16),
        "norm_mlp": norm_mlp[l].reshape(1, D_MODEL),
        "w_ff1": w_ff1[l].astype(_BF16),
        "w_ff2": w_ff2[l].astype(_BF16),
        "norm_final": norm_final.reshape(1, D_MODEL),
    }
    bias_rows = _bias_rows(rpb[l])
    run = functools.partial(_layer, wts=wts, bias_rows=bias_rows, tm_in=512, tm_mlp=512, rb=16, unroll=14)
    return (run(x_prompt), run(x_sample))
```

```python
import functools

import jax
import jax.numpy as jnp
from jax import lax
from jax.experimental import pallas as pl
from jax.experimental.pallas import tpu as pltpu

D_MODEL = 1024
GRID_W = 64
POOL_WIDTH = 512
POOL_GROUPS = 4
POOL_GROUP_W = POOL_WIDTH // POOL_GROUPS
POOL_WINDOWS = (2, 4, 8, 16)
POOL_HALO = max(POOL_WINDOWS) // 2
NA_HEADS = 8
NA_HEAD_DIM = 64
NA_WIDTH = NA_HEADS * NA_HEAD_DIM
NA_ROWS = 8
NA_COLS = 16
NA_KEYS = NA_ROWS * GRID_W
HEAD_PAIRS = NA_HEADS // 2
ALL_HEAD_PAIRS = tuple(range(HEAD_PAIRS))
N_ROW_OFFSETS = 2 * NA_ROWS - 1
N_BIAS_TILES = N_ROW_OFFSETS - 1
D_FF = 4 * D_MODEL
FF_CHUNK = 1024
MLP_SUB_TILE = 256
GATE_WIDTH = 2 * D_MODEL
IN_WIDTH = POOL_WIDTH + 3 * NA_WIDTH + GATE_WIDTH
RMS_EPS = 1e-6
NEG_INF = -1e30
LOG2_E = 1.4426950408889634
QK_SCALE = NA_HEAD_DIM ** -0.5 * LOG2_E

LANES = 128
VMEM_LIMIT_BYTES = 56 * 1024 * 1024

_BF16 = jnp.bfloat16
_F32 = jnp.float32


def _rms(x, g):
    var = jnp.mean(x * x, axis=-1, keepdims=True)
    return x * lax.rsqrt(var + RMS_EPS) * g


def _const_spec(shape):
    zeros = (0,) * len(shape)
    return pl.BlockSpec(shape, lambda *_: zeros, pipeline_mode=pl.Buffered(1))


def _inproj_kernel(x_ref, g_ref, w_ref, b_ref, p_ref, q_ref, k_ref, v_ref, gate_ref):
    xn = _rms(x_ref[...], g_ref[...]).astype(_BF16)

    def proj(lo, width):
        return jnp.dot(xn, w_ref[:, lo:lo + width], preferred_element_type=_F32)

    g0 = POOL_WIDTH + 3 * NA_WIDTH
    for c in range(GATE_WIDTH // 512):
        sl = slice(512 * c, 512 * (c + 1))
        z = proj(g0 + 512 * c, 512) + b_ref[:, sl]
        gate_ref[:, sl] = (0.5 * jnp.tanh(0.5 * z) + 0.5).astype(_BF16)
    p_ref[...] = proj(0, POOL_WIDTH)
    q_ref[...] = (proj(POOL_WIDTH, NA_WIDTH) * QK_SCALE).astype(_BF16)
    k_ref[...] = proj(POOL_WIDTH + NA_WIDTH, NA_WIDTH).astype(_BF16)
    v_ref[...] = proj(POOL_WIDTH + 2 * NA_WIDTH, NA_WIDTH).astype(_BF16)


def _inproj(x2d, norm_g, w_in, b_gate, *, tm):
    n = x2d.shape[0]
    row = lambda w: pl.BlockSpec((tm, w), lambda i: (i, 0))
    return pl.pallas_call(
        _inproj_kernel,
        grid=(n // tm,),
        in_specs=[row(D_MODEL), _const_spec((1, D_MODEL)), _const_spec((D_MODEL, IN_WIDTH)),
                  _const_spec((1, GATE_WIDTH))],
        out_specs=[row(POOL_WIDTH), row(NA_WIDTH), row(NA_WIDTH), row(NA_WIDTH), row(GATE_WIDTH)],
        out_shape=[jax.ShapeDtypeStruct((n, POOL_WIDTH), _F32),
                   jax.ShapeDtypeStruct((n, NA_WIDTH), _BF16),
                   jax.ShapeDtypeStruct((n, NA_WIDTH), _BF16),
                   jax.ShapeDtypeStruct((n, NA_WIDTH), _BF16),
                   jax.ShapeDtypeStruct((n, GATE_WIDTH), _BF16)],
        compiler_params=pltpu.CompilerParams(
            dimension_semantics=("arbitrary",), vmem_limit_bytes=VMEM_LIMIT_BYTES),
        name="inproj",
    )(x2d, norm_g, w_in, b_gate)


def _bias_rows(rpb):
    lo = GRID_W - NA_COLS
    ext = jnp.pad(rpb.astype(_F32) * LOG2_E, ((0, 0), (0, 0), (lo, LANES - lo - (2 * NA_COLS - 1))), mode="edge")
    return ext.reshape(NA_HEADS * N_ROW_OFFSETS, LANES)


def _build_bias_tiles(rext_ref, tb_ref):
    qc = lax.broadcasted_iota(jnp.int32, (GRID_W, LANES), 0)
    lane = lax.broadcasted_iota(jnp.int32, (GRID_W, LANES), 1)
    kc = lane & (GRID_W - 1)
    cs = jnp.clip(qc - NA_COLS // 2, 0, GRID_W - NA_COLS)
    valid = (kc >= cs) & (kc < cs + NA_COLS)
    left = lane < GRID_W
    for h in range(NA_HEADS):
        rows = slice((h % 2) * GRID_W, (h % 2 + 1) * GRID_W)

        def shifted(j, first_lane):
            row = jnp.broadcast_to(rext_ref[pl.ds(h * N_ROW_OFFSETS + j, 1), :], (GRID_W, LANES))
            return pltpu.roll(row, (first_lane + GRID_W + 1) % LANES, 1, stride=1, stride_axis=0)

        for j in range(N_BIAS_TILES):
            tile = jnp.where(left, shifted(j, 0), shifted(j + 1, GRID_W))
            tb_ref[h // 2, j, rows, :] = jnp.where(valid, tile, NEG_INF)


def _attn_kernel(q_ref, k_ref, v_ref, rext_ref, o_ref, s0_ref, s1_ref, p0_ref, p1_ref, m0_ref, m1_ref,
                 vones_ref, tb_ref, *, rows, rb, unroll, span):
    blk = pl.program_id(1)
    even_head = lax.broadcasted_iota(jnp.int32, (GRID_W, LANES), 1) < NA_HEAD_DIM
    s_slots, p_slots, m_slots = (s0_ref, s1_ref), (p0_ref, p1_ref), (m0_ref, m1_ref)
    lane_tiles = [slice(m * LANES, (m + 1) * LANES) for m in range(NA_KEYS // LANES)]
    base_row = jnp.clip(blk * rb - NA_ROWS // 2, 0, rows - span)

    @pl.when((pl.program_id(0) == 0) & (blk == 0))
    def _():
        vones_ref[:, :, LANES:] = jnp.ones((HEAD_PAIRS, span * GRID_W, LANES), _BF16)
        _build_bias_tiles(rext_ref, tb_ref)

    v0 = pl.multiple_of(base_row * GRID_W, GRID_W)
    for hp in ALL_HEAD_PAIRS:
        vones_ref[hp, :, :LANES] = v_ref[0, pl.ds(v0, span * GRID_W), hp * LANES:(hp + 1) * LANES]

    def key_rows(j):
        r = blk * rb + j
        rs = jnp.clip(r - NA_ROWS // 2, 0, rows - NA_ROWS)
        return rs, r - rs

    def scores(j, slot, hps=ALL_HEAD_PAIRS):
        rs, delta = key_rows(j)
        k0 = pl.multiple_of(rs * GRID_W, GRID_W)
        q0 = pl.multiple_of(j * GRID_W, GRID_W)
        for hp in hps:
            ls = slice(hp * LANES, (hp + 1) * LANES)
            qp = q_ref[0, pl.ds(q0, GRID_W), ls]
            zero = jnp.zeros_like(qp)
            q2 = jnp.concatenate([jnp.where(even_head, qp, zero), jnp.where(even_head, zero, qp)], axis=0)
            kp = k_ref[0, pl.ds(k0, NA_KEYS), ls]
            qk = lax.dot_general(q2, kp, (((1,), (1,)), ((), ())), preferred_element_type=_F32)
            s = [qk[:, lt] + tb_ref[hp, 2 * m - delta + (NA_ROWS - 1)] for m, lt in enumerate(lane_tiles)]
            for sm, lt in zip(s, lane_tiles):
                s_slots[slot][hp, :, lt] = sm
            mx = jnp.max(functools.reduce(jnp.maximum, s), axis=-1, keepdims=True)
            m_slots[slot][hp] = jnp.broadcast_to(mx, (2 * GRID_W, LANES))

    def softmax(j, slot, hps=ALL_HEAD_PAIRS):
        for hp in hps:
            mx = m_slots[slot][hp]
            for lt in lane_tiles:
                p_slots[slot][hp, :, lt] = jnp.exp2(s_slots[slot][hp, :, lt] - mx).astype(_BF16)

    def weighted_values(j, slot, hps=ALL_HEAD_PAIRS):
        rs, _ = key_rows(j)
        k0 = pl.multiple_of((rs - base_row) * GRID_W, GRID_W)
        q0 = pl.multiple_of(j * GRID_W, GRID_W)
        for hp in hps:
            ls = slice(hp * LANES, (hp + 1) * LANES)
            vo = vones_ref[hp, pl.ds(k0, NA_KEYS), :]
            o2l = jnp.dot(p_slots[slot][hp], vo, preferred_element_type=_F32)
            o2 = o2l[:, :LANES] / o2l[:, LANES:]
            out = jnp.where(even_head, o2[:GRID_W], o2[GRID_W:])
            o_ref[0, pl.ds(q0, GRID_W), ls] = out.astype(_BF16)

    scores(0, 0)
    scores(1, 1)
    softmax(0, 0)

    def pipeline_steps(u, carry):
        for i in range(unroll):
            t = 1 + unroll * u + i
            scores(t + 1, i % 2)
            softmax(t, (1 + i) % 2)
            weighted_values(t - 1, i % 2)
        return carry

    trips = (rb - 2) // unroll
    if trips == 1:
        pipeline_steps(0, 0)
    else:
        lax.fori_loop(0, trips, pipeline_steps, 0)
    softmax(rb - 1, 1)
    weighted_values(rb - 2, 0)
    weighted_values(rb - 1, 1)


def _attention(q, k, v, bias_rows, *, rb, unroll):
    b, s, _ = q.shape
    rows = s // GRID_W
    assert unroll % 2 == 0 and (rb - 2) % unroll == 0 and rows % rb == 0
    tq = rb * GRID_W
    qspec = pl.BlockSpec((1, tq, NA_WIDTH), lambda bi, i: (bi, i, 0))
    kvspec = pl.BlockSpec((1, s, NA_WIDTH), lambda bi, i: (bi, 0, 0))
    pair_rows = 2 * GRID_W
    span = min(rb + NA_ROWS, rows)
    return pl.pallas_call(
        functools.partial(_attn_kernel, rows=rows, rb=rb, unroll=unroll, span=span),
        grid=(b, rows // rb),
        in_specs=[qspec, kvspec, kvspec, _const_spec(bias_rows.shape)],
        out_specs=qspec,
        out_shape=jax.ShapeDtypeStruct((b, s, NA_WIDTH), _BF16),
        scratch_shapes=[pltpu.VMEM((HEAD_PAIRS, pair_rows, NA_KEYS), _F32)] * 2
        + [pltpu.VMEM((HEAD_PAIRS, pair_rows, NA_KEYS), _BF16)] * 2
        + [pltpu.VMEM((HEAD_PAIRS, pair_rows, LANES), _F32)] * 2
        + [pltpu.VMEM((HEAD_PAIRS, span * GRID_W, 2 * LANES), _BF16),
           pltpu.VMEM((HEAD_PAIRS, N_BIAS_TILES, pair_rows, LANES), _F32)],
        compiler_params=pltpu.CompilerParams(
            dimension_semantics=("arbitrary", "arbitrary"), vmem_limit_bytes=VMEM_LIMIT_BYTES),
        name="natten",
    )(q, k, v, bias_rows)


def _mix_mlp_kernel(x_ref, p_ref, pprev_ref, pnext_ref, gate_ref, att_ref,
                    wgrp_ref, pscale_ref, wpool_ref, wna_ref, wout_ref,
                    nmlp_ref, w1_ref, w2_ref, nfin_ref, out_ref, ext_ref, *, tm, sub, seq):
    s0 = (pl.program_id(0) * tm) % seq
    ext_ref[0:POOL_HALO, :] = jnp.where(s0 == 0, 0.0, pprev_ref[...])
    ext_ref[POOL_HALO:POOL_HALO + tm, :] = p_ref[...]
    ext_ref[POOL_HALO + tm:, :] = jnp.where(s0 + tm == seq, 0.0, pnext_ref[...])

    def window_mean(tot, r0, w):
        h = w // 2
        mean = tot * (1.0 / w)

        def cut_rows(lo):
            pos = ((s0 + r0 + lo).astype(_F32)
                   + lax.broadcasted_iota(jnp.int32, (POOL_HALO, POOL_GROUP_W), 0).astype(_F32))
            cnt = jnp.minimum(pos + float(h), float(seq)) - jnp.maximum(pos - float(h), 0.0)
            return tot[lo:lo + POOL_HALO] / cnt

        if r0 == 0:
            mean = jnp.concatenate([cut_rows(0), mean[POOL_HALO:]], axis=0)
        if r0 + sub == tm:
            mean = jnp.concatenate([mean[:sub - POOL_HALO], cut_rows(sub - POOL_HALO)], axis=0)
        return mean

    def mixer(r0):
        rows = slice(r0, r0 + sub)
        mixed = []
        for g, w in enumerate(POOL_WINDOWS):
            ls = slice(g * POOL_GROUP_W, (g + 1) * POOL_GROUP_W)
            h = w // 2
            run = ext_ref[POOL_HALO + r0 - h:POOL_HALO + r0 + sub + h, ls]
            width = 1
            while 2 * width < w:
                n = run.shape[0] - width
                run = run[:n] + run[width:]
                width *= 2
            tot = run[:sub] + run[width:width + sub]
            pooled = window_mean(tot, r0, w) - p_ref[rows, ls]
            mixed.append(jnp.dot(pooled.astype(_BF16), wgrp_ref[g], preferred_element_type=_F32))
        mixed = jnp.concatenate(mixed, axis=1) * pscale_ref[...]
        pool_out = jnp.dot(mixed.astype(_BF16), wpool_ref[...], preferred_element_type=_F32)
        na_out = jnp.dot(att_ref[rows, :], wna_ref[...], preferred_element_type=_F32)
        merged = gate_ref[rows, :D_MODEL] * pool_out + gate_ref[rows, D_MODEL:] * na_out
        return x_ref[rows, :] + jnp.dot(merged.astype(_BF16), wout_ref[...], preferred_element_type=_F32)

    def mlp(r0, x1):
        xn = _rms(x1, nmlp_ref[...]).astype(_BF16)
        y = x1
        for c in range(D_FF // FF_CHUNK):
            cs = slice(c * FF_CHUNK, (c + 1) * FF_CHUNK)
            hid = jnp.square(jnp.maximum(jnp.dot(xn, w1_ref[:, cs], preferred_element_type=_F32), 0.0))
            y = y + jnp.dot(hid.astype(_BF16), w2_ref[cs, :], preferred_element_type=_F32)
        out_ref[r0:r0 + sub, :] = _rms(y, nfin_ref[...])

    starts = range(0, tm, sub)
    x1s = [mixer(r0) for r0 in starts]
    for r0, x1 in zip(starts, x1s):
        mlp(r0, x1)


def _mix_mlp(x2d, p, gates, att, wgrp, pscale, wpool, wna, wout, nmlp, w1, w2, nfin, *, tm, seq):
    n = x2d.shape[0]
    assert tm % MLP_SUB_TILE == 0 and seq % tm == 0
    hb = tm // POOL_HALO
    n_hb = n // POOL_HALO
    row = lambda w: pl.BlockSpec((tm, w), lambda i: (i, 0))
    prev = pl.BlockSpec((POOL_HALO, POOL_WIDTH), lambda i: (jnp.maximum(i * hb - 1, 0), 0))
    nxt = pl.BlockSpec((POOL_HALO, POOL_WIDTH), lambda i: (jnp.minimum((i + 1) * hb, n_hb - 1), 0))
    return pl.pallas_call(
        functools.partial(_mix_mlp_kernel, tm=tm, sub=MLP_SUB_TILE, seq=seq),
        grid=(n // tm,),
        in_specs=[row(D_MODEL), row(POOL_WIDTH), prev, nxt, row(GATE_WIDTH), row(NA_WIDTH),
                  _const_spec(wgrp.shape), _const_spec(pscale.shape), _const_spec(wpool.shape),
                  _const_spec(wna.shape), _const_spec(wout.shape), _const_spec(nmlp.shape),
                  _const_spec(w1.shape), _const_spec(w2.shape), _const_spec(nfin.shape)],
        out_specs=row(D_MODEL),
        out_shape=jax.ShapeDtypeStruct((n, D_MODEL), _F32),
        scratch_shapes=[pltpu.VMEM((tm + 2 * POOL_HALO, POOL_WIDTH), _F32)],
        compiler_params=pltpu.CompilerParams(
            dimension_semantics=("arbitrary",), vmem_limit_bytes=VMEM_LIMIT_BYTES),
        name="mix_mlp",
    )(x2d, p, p, p, gates, att, wgrp, pscale, wpool, wna, wout, nmlp, w1, w2, nfin)


def _layer(x, wts, bias_rows, *, tm_in, tm_mlp, rb, unroll):
    b, s, d = x.shape
    x2d = x.reshape(b * s, d)
    p, q, k, v, gates = _inproj(x2d, wts["norm_mix"], wts["w_in"], wts["b_gate"], tm=tm_in)
    shp = (b, s, NA_WIDTH)
    att = _attention(q.reshape(shp), k.reshape(shp), v.reshape(shp), bias_rows, rb=rb, unroll=unroll)
    y = _mix_mlp(x2d, p, gates, att.reshape(b * s, NA_WIDTH), wts["w_pool_grp"], wts["pool_scale"],
                 wts["w_pool_proj"], wts["w_na_proj"], wts["w_out"], wts["norm_mlp"],
                 wts["w_ff1"], wts["w_ff2"], wts["norm_final"], tm=tm_mlp, seq=s)
    return y.reshape(b, s, d)


def kernel(x_prompt, x_sample, norm_mix, w_in, b_gate, w_pool_grp, pool_scale, w_pool_proj, rpb,
           w_na_proj, w_out, norm_mlp, w_ff1, w_ff2, norm_final):
    depth = norm_mix.shape[0]
    assert depth == 1
    l = 0
    wts = {
        "norm_mix": norm_mix[l].reshape(1, D_MODEL),
        "w_in": w_in[l].astype(_BF16),
        "b_gate": b_gate[l].reshape(1, GATE_WIDTH),
        "w_pool_grp": w_pool_grp[l].astype(_BF16),
        "pool_scale": pool_scale[l].reshape(1, POOL_WIDTH),
        "w_pool_proj": w_pool_proj[l].astype(_BF16),
        "w_na_proj": w_na_proj[l].astype(_BF16),
        "w_out": w_out[l].astype(_BF16),
        "norm_mlp": norm_mlp[l].reshape(1, D_MODEL),
        "w_ff1": w_ff1[l].astype(_BF16),
        "w_ff2": w_ff2[l].astype(_BF16),
        "norm_final": norm_final.reshape(1, D_MODEL),
    }
    bias_rows = _bias_rows(rpb[l])
    run = functools.partial(_layer, wts=wts, bias_rows=bias_rows, tm_in=512, tm_mlp=512, rb=16, unroll=14)
    return (run(x_prompt), run(x_sample))
```

```python
import functools

import jax
import jax.numpy as jnp
from jax import lax
from jax.experimental import pallas as pl
from jax.experimental.pallas import tpu as pltpu

D_MODEL = 1024
GRID_W = 64
POOL_WIDTH = 512
POOL_GROUPS = 4
POOL_GROUP_W = POOL_WIDTH // POOL_GROUPS
POOL_WINDOWS = (2, 4, 8, 16)
POOL_HALO = max(POOL_WINDOWS) // 2
NA_HEADS = 8
NA_HEAD_DIM = 64
NA_WIDTH = NA_HEADS * NA_HEAD_DIM
NA_ROWS = 8
NA_COLS = 16
NA_KEYS = NA_ROWS * GRID_W
HEAD_PAIRS = NA_HEADS // 2
ALL_HEAD_PAIRS = tuple(range(HEAD_PAIRS))
N_ROW_OFFSETS = 2 * NA_ROWS - 1
N_BIAS_TILES = N_ROW_OFFSETS - 1
D_FF = 4 * D_MODEL
FF_CHUNK = 1024
MLP_SUB_TILE = 256
GATE_WIDTH = 2 * D_MODEL
IN_WIDTH = POOL_WIDTH + 3 * NA_WIDTH + GATE_WIDTH
RMS_EPS = 1e-6
NEG_INF = -1e30
LOG2_E = 1.4426950408889634
QK_SCALE = NA_HEAD_DIM ** -0.5 * LOG2_E

LANES = 128
VMEM_LIMIT_BYTES = 56 * 1024 * 1024

_BF16 = jnp.bfloat16
_F32 = jnp.float32


def _rms(x, g):
    var = jnp.mean(x * x, axis=-1, keepdims=True)
    return x * lax.rsqrt(var + RMS_EPS) * g


def _const_spec(shape):
    zeros = (0,) * len(shape)
    return pl.BlockSpec(shape, lambda *_: zeros, pipeline_mode=pl.Buffered(1))


def _inproj_kernel(x_ref, g_ref, w_ref, b_ref, p_ref, q_ref, k_ref, v_ref, gate_ref):
    xn = _rms(x_ref[...], g_ref[...]).astype(_BF16)

    def proj(lo, width):
        return jnp.dot(xn, w_ref[:, lo:lo + width], preferred_element_type=_F32)

    g0 = POOL_WIDTH + 3 * NA_WIDTH
    for c in range(GATE_WIDTH // 512):
        sl = slice(512 * c, 512 * (c + 1))
        z = proj(g0 + 512 * c, 512) + b_ref[:, sl]
        gate_ref[:, sl] = (0.5 * jnp.tanh(0.5 * z) + 0.5).astype(_BF16)
    p_ref[...] = proj(0, POOL_WIDTH)
    q_ref[...] = (proj(POOL_WIDTH, NA_WIDTH) * QK_SCALE).astype(_BF16)
    k_ref[...] = proj(POOL_WIDTH + NA_WIDTH, NA_WIDTH).astype(_BF16)
    v_ref[...] = proj(POOL_WIDTH + 2 * NA_WIDTH, NA_WIDTH).astype(_BF16)


def _inproj(x2d, norm_g, w_in, b_gate, *, tm):
    n = x2d.shape[0]
    row = lambda w: pl.BlockSpec((tm, w), lambda i: (i, 0))
    return pl.pallas_call(
        _inproj_kernel,
        grid=(n // tm,),
        in_specs=[row(D_MODEL), _const_spec((1, D_MODEL)), _const_spec((D_MODEL, IN_WIDTH)),
                  _const_spec((1, GATE_WIDTH))],
        out_specs=[row(POOL_WIDTH), row(NA_WIDTH), row(NA_WIDTH), row(NA_WIDTH), row(GATE_WIDTH)],
        out_shape=[jax.ShapeDtypeStruct((n, POOL_WIDTH), _F32),
                   jax.ShapeDtypeStruct((n, NA_WIDTH), _BF16),
                   jax.ShapeDtypeStruct((n, NA_WIDTH), _BF16),
                   jax.ShapeDtypeStruct((n, NA_WIDTH), _BF16),
                   jax.ShapeDtypeStruct((n, GATE_WIDTH), _BF16)],
        compiler_params=pltpu.CompilerParams(
            dimension_semantics=("arbitrary",), vmem_limit_bytes=VMEM_LIMIT_BYTES),
        name="inproj",
    )(x2d, norm_g, w_in, b_gate)


def _bias_rows(rpb):
    lo = GRID_W - NA_COLS
    ext = jnp.pad(rpb.astype(_F32) * LOG2_E, ((0, 0), (0, 0), (lo, LANES - lo - (2 * NA_COLS - 1))), mode="edge")
    return ext.reshape(NA_HEADS * N_ROW_OFFSETS, LANES)


def _build_bias_tiles(rext_ref, tb_ref):
    qc = lax.broadcasted_iota(jnp.int32, (GRID_W, LANES), 0)
    lane = lax.broadcasted_iota(jnp.int32, (GRID_W, LANES), 1)
    kc = lane & (GRID_W - 1)
    cs = jnp.clip(qc - NA_COLS // 2, 0, GRID_W - NA_COLS)
    valid = (kc >= cs) & (kc < cs + NA_COLS)
    left = lane < GRID_W
    for h in range(NA_HEADS):
        rows = slice((h % 2) * GRID_W, (h % 2 + 1) * GRID_W)

        def shifted(j, first_lane):
            row = jnp.broadcast_to(rext_ref[pl.ds(h * N_ROW_OFFSETS + j, 1), :], (GRID_W, LANES))
            return pltpu.roll(row, (first_lane + GRID_W + 1) % LANES, 1, stride=1, stride_axis=0)

        for j in range(N_BIAS_TILES):
            tile = jnp.where(left, shifted(j, 0), shifted(j + 1, GRID_W))
            tb_ref[h // 2, j, rows, :] = jnp.where(valid, tile, NEG_INF)


def _attn_kernel(q_ref, k_ref, v_ref, rext_ref, o_ref, s0_ref, s1_ref, p0_ref, p1_ref, m0_ref, m1_ref,
                 vones_ref, tb_ref, *, rows, rb, unroll, span):
    blk = pl.program_id(1)
    even_head = lax.broadcasted_iota(jnp.int32, (GRID_W, LANES), 1) < NA_HEAD_DIM
    s_slots, p_slots, m_slots = (s0_ref, s1_ref), (p0_ref, p1_ref), (m0_ref, m1_ref)
    lane_tiles = [slice(m * LANES, (m + 1) * LANES) for m in range(NA_KEYS // LANES)]
    base_row = jnp.clip(blk * rb - NA_ROWS // 2, 0, rows - span)

    @pl.when((pl.program_id(0) == 0) & (blk == 0))
    def _():
        vones_ref[:, :, LANES:] = jnp.ones((HEAD_PAIRS, span * GRID_W, LANES), _BF16)
        _build_bias_tiles(rext_ref, tb_ref)

    v0 = pl.multiple_of(base_row * GRID_W, GRID_W)
    for hp in ALL_HEAD_PAIRS:
        vones_ref[hp, :, :LANES] = v_ref[0, pl.ds(v0, span * GRID_W), hp * LANES:(hp + 1) * LANES]

    def key_rows(j):
        r = blk * rb + j
        rs = jnp.clip(r - NA_ROWS // 2, 0, rows - NA_ROWS)
        return rs, r - rs

    def scores(j, slot, hps=ALL_HEAD_PAIRS):
        rs, delta = key_rows(j)
        k0 = pl.multiple_of(rs * GRID_W, GRID_W)
        q0 = pl.multiple_of(j * GRID_W, GRID_W)
        for hp in hps:
            ls = slice(hp * LANES, (hp + 1) * LANES)
            qp = q_ref[0, pl.ds(q0, GRID_W), ls]
            zero = jnp.zeros_like(qp)
            q2 = jnp.concatenate([jnp.where(even_head, qp, zero), jnp.where(even_head, zero, qp)], axis=0)
            kp = k_ref[0, pl.ds(k0, NA_KEYS), ls]
            qk = lax.dot_general(q2, kp, (((1,), (1,)), ((), ())), preferred_element_type=_F32)
            s = [qk[:, lt] + tb_ref[hp, 2 * m - delta + (NA_ROWS - 1)] for m, lt in enumerate(lane_tiles)]
            for sm, lt in zip(s, lane_tiles):
                s_slots[slot][hp, :, lt] = sm
            mx = jnp.max(functools.reduce(jnp.maximum, s), axis=-1, keepdims=True)
            m_slots[slot][hp] = jnp.broadcast_to(mx, (2 * GRID_W, LANES))

    def softmax(j, slot, hps=ALL_HEAD_PAIRS):
        for hp in hps:
            mx = m_slots[slot][hp]
            for lt in lane_tiles:
                p_slots[slot][hp, :, lt] = jnp.exp2(s_slots[slot][hp, :, lt] - mx).astype(_BF16)

    def weighted_values(j, slot, hps=ALL_HEAD_PAIRS):
        rs, _ = key_rows(j)
        k0 = pl.multiple_of((rs - base_row) * GRID_W, GRID_W)
        q0 = pl.multiple_of(j * GRID_W, GRID_W)
        for hp in hps:
            ls = slice(hp * LANES, (hp + 1) * LANES)
            vo = vones_ref[hp, pl.ds(k0, NA_KEYS), :]
            o2l = jnp.dot(p_slots[slot][hp], vo, preferred_element_type=_F32)
            o2 = o2l[:, :LANES] / o2l[:, LANES:]
            out = jnp.where(even_head, o2[:GRID_W], o2[GRID_W:])
            o_ref[0, pl.ds(q0, GRID_W), ls] = out.astype(_BF16)

    scores(0, 0)
    scores(1, 1)
    softmax(0, 0)

    def pipeline_steps(u, carry):
        for i in range(unroll):
            t = 1 + unroll * u + i
            scores(t + 1, i % 2)
            softmax(t, (1 + i) % 2)
            weighted_values(t - 1, i % 2)
        return carry

    trips = (rb - 2) // unroll
    if trips == 1:
        pipeline_steps(0, 0)
    else:
        lax.fori_loop(0, trips, pipeline_steps, 0)
    softmax(rb - 1, 1)
    weighted_values(rb - 2, 0)
    weighted_values(rb - 1, 1)


def _attention(q, k, v, bias_rows, *, rb, unroll):
    b, s, _ = q.shape
    rows = s // GRID_W
    assert unroll % 2 == 0 and (rb - 2) % unroll == 0 and rows % rb == 0
    tq = rb * GRID_W
    qspec = pl.BlockSpec((1, tq, NA_WIDTH), lambda bi, i: (bi, i, 0))
    kvspec = pl.BlockSpec((1, s, NA_WIDTH), lambda bi, i: (bi, 0, 0))
    pair_rows = 2 * GRID_W
    span = min(rb + NA_ROWS, rows)
    return pl.pallas_call(
        functools.partial(_attn_kernel, rows=rows, rb=rb, unroll=unroll, span=span),
        grid=(b, rows // rb),
        in_specs=[qspec, kvspec, kvspec, _const_spec(bias_rows.shape)],
        out_specs=qspec,
        out_shape=jax.ShapeDtypeStruct((b, s, NA_WIDTH), _BF16),
        scratch_shapes=[pltpu.VMEM((HEAD_PAIRS, pair_rows, NA_KEYS), _F32)] * 2
        + [pltpu.VMEM((HEAD_PAIRS, pair_rows, NA_KEYS), _BF16)] * 2
        + [pltpu.VMEM((HEAD_PAIRS, pair_rows, LANES), _F32)] * 2
        + [pltpu.VMEM((HEAD_PAIRS, span * GRID_W, 2 * LANES), _BF16),
           pltpu.VMEM((HEAD_PAIRS, N_BIAS_TILES, pair_rows, LANES), _F32)],
        compiler_params=pltpu.CompilerParams(
            dimension_semantics=("arbitrary", "arbitrary"), vmem_limit_bytes=VMEM_LIMIT_BYTES),
        name="natten",
    )(q, k, v, bias_rows)


def _mix_mlp_kernel(x_ref, p_ref, pprev_ref, pnext_ref, gate_ref, att_ref,
                    wgrp_ref, pscale_ref, wpool_ref, wna_ref, wout_ref,
                    nmlp_ref, w1_ref, w2_ref, nfin_ref, out_ref, ext_ref, *, tm, sub, seq):
    s0 = (pl.program_id(0) * tm) % seq
    ext_ref[0:POOL_HALO, :] = jnp.where(s0 == 0, 0.0, pprev_ref[...])
    ext_ref[POOL_HALO:POOL_HALO + tm, :] = p_ref[...]
    ext_ref[POOL_HALO + tm:, :] = jnp.where(s0 + tm == seq, 0.0, pnext_ref[...])

    def window_mean(tot, r0, w):
        h = w // 2
        mean = tot * (1.0 / w)

        def cut_rows(lo):
            pos = ((s0 + r0 + lo).astype(_F32)
                   + lax.broadcasted_iota(jnp.int32, (POOL_HALO, POOL_GROUP_W), 0).astype(_F32))
            cnt = jnp.minimum(pos + float(h), float(seq)) - jnp.maximum(pos - float(h), 0.0)
            return tot[lo:lo + POOL_HALO] / cnt

        if r0 == 0:
            mean = jnp.concatenate([cut_rows(0), mean[POOL_HALO:]], axis=0)
        if r0 + sub == tm:
            mean = jnp.concatenate([mean[:sub - POOL_HALO], cut_rows(sub - POOL_HALO)], axis=0)
        return mean

    def mixer(r0):
        rows = slice(r0, r0 + sub)
        mixed = []
        for g, w in enumerate(POOL_WINDOWS):
            ls = slice(g * POOL_GROUP_W, (g + 1) * POOL_GROUP_W)
            h = w // 2
            run = ext_ref[POOL_HALO + r0 - h:POOL_HALO + r0 + sub + h, ls]
            width = 1
            while 2 * width < w:
                n = run.shape[0] - width
                run = run[:n] + run[width:]
                width *= 2
            tot = run[:sub] + run[width:width + sub]
            pooled = window_mean(tot, r0, w) - p_ref[rows, ls]
            mixed.append(jnp.dot(pooled.astype(_BF16), wgrp_ref[g], preferred_element_type=_F32))
        mixed = jnp.concatenate(mixed, axis=1) * pscale_ref[...]
        pool_out = jnp.dot(mixed.astype(_BF16), wpool_ref[...], preferred_element_type=_F32)
        na_out = jnp.dot(att_ref[rows, :], wna_ref[...], preferred_element_type=_F32)
        merged = gate_ref[rows, :D_MODEL] * pool_out + gate_ref[rows, D_MODEL:] * na_out
        return x_ref[rows, :] + jnp.dot(merged.astype(_BF16), wout_ref[...], preferred_element_type=_F32)

    def mlp(r0, x1):
        xn = _rms(x1, nmlp_ref[...]).astype(_BF16)
        y = x1
        for c in range(D_FF // FF_CHUNK):
            cs = slice(c * FF_CHUNK, (c + 1) * FF_CHUNK)
            hid = jnp.square(jnp.maximum(jnp.dot(xn, w1_ref[:, cs], preferred_element_type=_F32), 0.0))
            y = y + jnp.dot(hid.astype(_BF16), w2_ref[cs, :], preferred_element_type=_F32)
        out_ref[r0:r0 + sub, :] = _rms(y, nfin_ref[...])

    starts = range(0, tm, sub)
    x1s = [mixer(r0) for r0 in starts]
    for r0, x1 in zip(starts, x1s):
        mlp(r0, x1)


def _mix_mlp(x2d, p, gates, att, wgrp, pscale, wpool, wna, wout, nmlp, w1, w2, nfin, *, tm, seq):
    n = x2d.shape[0]
    assert tm % MLP_SUB_TILE == 0 and seq % tm == 0
    hb = tm // POOL_HALO
    n_hb = n // POOL_HALO
    row = lambda w: pl.BlockSpec((tm, w), lambda i: (i, 0))
    prev = pl.BlockSpec((POOL_HALO, POOL_WIDTH), lambda i: (jnp.maximum(i * hb - 1, 0), 0))
    nxt = pl.BlockSpec((POOL_HALO, POOL_WIDTH), lambda i: (jnp.minimum((i + 1) * hb, n_hb - 1), 0))
    return pl.pallas_call(
        functools.partial(_mix_mlp_kernel, tm=tm, sub=MLP_SUB_TILE, seq=seq),
        grid=(n // tm,),
        in_specs=[row(D_MODEL), row(POOL_WIDTH), prev, nxt, row(GATE_WIDTH), row(NA_WIDTH),
                  _const_spec(wgrp.shape), _const_spec(pscale.shape), _const_spec(wpool.shape),
                  _const_spec(wna.shape), _const_spec(wout.shape), _const_spec(nmlp.shape),
                  _const_spec(w1.shape), _const_spec(w2.shape), _const_spec(nfin.shape)],
        out_specs=row(D_MODEL),
        out_shape=jax.ShapeDtypeStruct((n, D_MODEL), _F32),
        scratch_shapes=[pltpu.VMEM((tm + 2 * POOL_HALO, POOL_WIDTH), _F32)],
        compiler_params=pltpu.CompilerParams(
            dimension_semantics=("arbitrary",), vmem_limit_bytes=VMEM_LIMIT_BYTES),
        name="mix_mlp",
    )(x2d, p, p, p, gates, att, wgrp, pscale, wpool, wna, wout, nmlp, w1, w2, nfin)


def _layer(x, wts, bias_rows, *, tm_in, tm_mlp, rb, unroll):
    b, s, d = x.shape
    x2d = x.reshape(b * s, d)
    p, q, k, v, gates = _inproj(x2d, wts["norm_mix"], wts["w_in"], wts["b_gate"], tm=tm_in)
    shp = (b, s, NA_WIDTH)
    att = _attention(q.reshape(shp), k.reshape(shp), v.reshape(shp), bias_rows, rb=rb, unroll=unroll)
    y = _mix_mlp(x2d, p, gates, att.reshape(b * s, NA_WIDTH), wts["w_pool_grp"], wts["pool_scale"],
                 wts["w_pool_proj"], wts["w_na_proj"], wts["w_out"], wts["norm_mlp"],
                 wts["w_ff1"], wts["w_ff2"], wts["norm_final"], tm=tm_mlp, seq=s)
    return y.reshape(b, s, d)


def kernel(x_prompt, x_sample, norm_mix, w_in, b_gate, w_pool_grp, pool_scale, w_pool_proj, rpb,
           w_na_proj, w_out, norm_mlp, w_ff1, w_ff2, norm_final):
    depth = norm_mix.shape[0]
    assert depth == 1
    l = 0
    wts = {
        "norm_mix": norm_mix[l].reshape(1, D_MODEL),
        "w_in": w_in[l].astype(_BF16),
        "b_gate": b_gate[l].reshape(1, GATE_WIDTH),
        "w_pool_grp": w_pool_grp[l].astype(_BF16),
        "pool_scale": pool_scale[l].reshape(1, POOL_WIDTH),
        "w_pool_proj": w_pool_proj[l].astype(_BF16),
        "w_na_proj": w_na_proj[l].astype(_BF16),
        "w_out": w_out[l].astype(_BF16),
        "norm_mlp": norm_mlp[l].reshape(1, D_MODEL),
        "w_ff1": w_ff1[l].astype(_BF16),
        "w_ff2": w_ff2[l].astype(_BF16),
        "norm_final": norm_final.reshape(1, D_MODEL),
    }
    bias_rows = _bias_rows(rpb[l])
    run = functools.partial(_layer, wts=wts, bias_rows=bias_rows, tm_in=1024, tm_mlp=512, rb=16, unroll=14)
    return (run(x_prompt), run(x_sample))
```

```python
import functools

import jax
import jax.numpy as jnp
from jax import lax
from jax.experimental import pallas as pl
from jax.experimental.pallas import tpu as pltpu

D_MODEL = 1024
GRID_W = 64
POOL_WIDTH = 512
POOL_GROUPS = 4
POOL_GROUP_W = POOL_WIDTH // POOL_GROUPS
POOL_WINDOWS = (2, 4, 8, 16)
POOL_HALO = max(POOL_WINDOWS) // 2
NA_HEADS = 8
NA_HEAD_DIM = 64
NA_WIDTH = NA_HEADS * NA_HEAD_DIM
NA_ROWS = 8
NA_COLS = 16
NA_KEYS = NA_ROWS * GRID_W
HEAD_PAIRS = NA_HEADS // 2
ALL_HEAD_PAIRS = tuple(range(HEAD_PAIRS))
N_ROW_OFFSETS = 2 * NA_ROWS - 1
N_BIAS_TILES = N_ROW_OFFSETS - 1
D_FF = 4 * D_MODEL
FF_CHUNK = 1024
MLP_SUB_TILE = 256
GATE_WIDTH = 2 * D_MODEL
IN_WIDTH = POOL_WIDTH + 3 * NA_WIDTH + GATE_WIDTH
RMS_EPS = 1e-6
NEG_INF = -1e30
LOG2_E = 1.4426950408889634
QK_SCALE = NA_HEAD_DIM ** -0.5 * LOG2_E

LANES = 128
VMEM_LIMIT_BYTES = 56 * 1024 * 1024

_BF16 = jnp.bfloat16
_F32 = jnp.float32


def _rms(x, g):
    var = jnp.mean(x * x, axis=-1, keepdims=True)
    return x * lax.rsqrt(var + RMS_EPS) * g


def _const_spec(shape):
    zeros = (0,) * len(shape)
    return pl.BlockSpec(shape, lambda *_: zeros, pipeline_mode=pl.Buffered(1))


def _inproj_kernel(x_ref, g_ref, w_ref, b_ref, p_ref, q_ref, k_ref, v_ref, gate_ref):
    xn = _rms(x_ref[...], g_ref[...]).astype(_BF16)

    def proj(lo, width):
        return jnp.dot(xn, w_ref[:, lo:lo + width], preferred_element_type=_F32)

    g0 = POOL_WIDTH + 3 * NA_WIDTH
    for c in range(GATE_WIDTH // 512):
        sl = slice(512 * c, 512 * (c + 1))
        z = proj(g0 + 512 * c, 512) + b_ref[:, sl]
        gate_ref[:, sl] = (0.5 * jnp.tanh(0.5 * z) + 0.5).astype(_BF16)
    p_ref[...] = proj(0, POOL_WIDTH)
    q_ref[...] = (proj(POOL_WIDTH, NA_WIDTH) * QK_SCALE).astype(_BF16)
    k_ref[...] = proj(POOL_WIDTH + NA_WIDTH, NA_WIDTH).astype(_BF16)
    v_ref[...] = proj(POOL_WIDTH + 2 * NA_WIDTH, NA_WIDTH).astype(_BF16)


def _inproj(x2d, norm_g, w_in, b_gate, *, tm):
    n = x2d.shape[0]
    row = lambda w: pl.BlockSpec((tm, w), lambda i: (i, 0))
    return pl.pallas_call(
        _inproj_kernel,
        grid=(n // tm,),
        in_specs=[row(D_MODEL), _const_spec((1, D_MODEL)), _const_spec((D_MODEL, IN_WIDTH)),
                  _const_spec((1, GATE_WIDTH))],
        out_specs=[row(POOL_WIDTH), row(NA_WIDTH), row(NA_WIDTH), row(NA_WIDTH), row(GATE_WIDTH)],
        out_shape=[jax.ShapeDtypeStruct((n, POOL_WIDTH), _F32),
                   jax.ShapeDtypeStruct((n, NA_WIDTH), _BF16),
                   jax.ShapeDtypeStruct((n, NA_WIDTH), _BF16),
                   jax.ShapeDtypeStruct((n, NA_WIDTH), _BF16),
                   jax.ShapeDtypeStruct((n, GATE_WIDTH), _BF16)],
        compiler_params=pltpu.CompilerParams(
            dimension_semantics=("arbitrary",), vmem_limit_bytes=VMEM_LIMIT_BYTES),
        name="inproj",
    )(x2d, norm_g, w_in, b_gate)


def _bias_rows(rpb):
    lo = GRID_W - NA_COLS
    ext = jnp.pad(rpb.astype(_F32) * LOG2_E, ((0, 0), (0, 0), (lo, LANES - lo - (2 * NA_COLS - 1))), mode="edge")
    return ext.reshape(NA_HEADS * N_ROW_OFFSETS, LANES)


def _build_bias_tiles(rext_ref, tb_ref):
    qc = lax.broadcasted_iota(jnp.int32, (GRID_W, LANES), 0)
    lane = lax.broadcasted_iota(jnp.int32, (GRID_W, LANES), 1)
    kc = lane & (GRID_W - 1)
    cs = jnp.clip(qc - NA_COLS // 2, 0, GRID_W - NA_COLS)
    valid = (kc >= cs) & (kc < cs + NA_COLS)
    left = lane < GRID_W
    for h in range(NA_HEADS):
        rows = slice((h % 2) * GRID_W, (h % 2 + 1) * GRID_W)

        def shifted(j, first_lane):
            row = jnp.broadcast_to(rext_ref[pl.ds(h * N_ROW_OFFSETS + j, 1), :], (GRID_W, LANES))
            return pltpu.roll(row, (first_lane + GRID_W + 1) % LANES, 1, stride=1, stride_axis=0)

        for j in range(N_BIAS_TILES):
            tile = jnp.where(left, shifted(j, 0), shifted(j + 1, GRID_W))
            tb_ref[h // 2, j, rows, :] = jnp.where(valid, tile, NEG_INF)


def _attn_kernel(q_ref, k_ref, v_ref, rext_ref, o_ref, s0_ref, s1_ref, p0_ref, p1_ref, m0_ref, m1_ref,
                 vones_ref, tb_ref, *, rows, rb, span):
    blk = pl.program_id(1)
    even_head = lax.broadcasted_iota(jnp.int32, (GRID_W, LANES), 1) < NA_HEAD_DIM
    s_slots, p_slots, m_slots = (s0_ref, s1_ref), (p0_ref, p1_ref), (m0_ref, m1_ref)
    lane_tiles = [slice(m * LANES, (m + 1) * LANES) for m in range(NA_KEYS // LANES)]
    base_row = jnp.clip(blk * rb - NA_ROWS // 2, 0, rows - span)

    @pl.when((pl.program_id(0) == 0) & (blk == 0))
    def _():
        vones_ref[:, :, LANES:] = jnp.ones((HEAD_PAIRS, span * GRID_W, LANES), _BF16)
        _build_bias_tiles(rext_ref, tb_ref)

    v0 = pl.multiple_of(base_row * GRID_W, GRID_W)
    for hp in ALL_HEAD_PAIRS:
        vones_ref[hp, :, :LANES] = v_ref[0, pl.ds(v0, span * GRID_W), hp * LANES:(hp + 1) * LANES]

    def key_rows(j):
        r = blk * rb + j
        rs = jnp.clip(r - NA_ROWS // 2, 0, rows - NA_ROWS)
        return rs, r - rs

    def scores(j, slot, hps=ALL_HEAD_PAIRS):
        rs, delta = key_rows(j)
        k0 = pl.multiple_of(rs * GRID_W, GRID_W)
        q0 = pl.multiple_of(j * GRID_W, GRID_W)
        for hp in hps:
            ls = slice(hp * LANES, (hp + 1) * LANES)
            qp = q_ref[0, pl.ds(q0, GRID_W), ls]
            zero = jnp.zeros_like(qp)
            q2 = jnp.concatenate([jnp.where(even_head, qp, zero), jnp.where(even_head, zero, qp)], axis=0)
            kp = k_ref[0, pl.ds(k0, NA_KEYS), ls]
            qk = lax.dot_general(q2, kp, (((1,), (1,)), ((), ())), preferred_element_type=_F32)
            s = [qk[:, lt] + tb_ref[hp, 2 * m - delta + (NA_ROWS - 1)] for m, lt in enumerate(lane_tiles)]
            for sm, lt in zip(s, lane_tiles):
                s_slots[slot][hp, :, lt] = sm
            mx = jnp.max(functools.reduce(jnp.maximum, s), axis=-1, keepdims=True)
            m_slots[slot][hp] = jnp.broadcast_to(mx, (2 * GRID_W, LANES))

    def softmax(j, slot, hps=ALL_HEAD_PAIRS):
        for hp in hps:
            mx = m_slots[slot][hp]
            for lt in lane_tiles:
                p_slots[slot][hp, :, lt] = jnp.exp2(s_slots[slot][hp, :, lt] - mx).astype(_BF16)

    def weighted_values(j, slot, hps=ALL_HEAD_PAIRS):
        rs, _ = key_rows(j)
        k0 = pl.multiple_of((rs - base_row) * GRID_W, GRID_W)
        q0 = pl.multiple_of(j * GRID_W, GRID_W)
        for hp in hps:
            ls = slice(hp * LANES, (hp + 1) * LANES)
            vo = vones_ref[hp, pl.ds(k0, NA_KEYS), :]
            o2l = jnp.dot(p_slots[slot][hp], vo, preferred_element_type=_F32)
            o2 = o2l[:, :LANES] / o2l[:, LANES:]
            out = jnp.where(even_head, o2[:GRID_W], o2[GRID_W:])
            o_ref[0, pl.ds(q0, GRID_W), ls] = out.astype(_BF16)

    for step in range(rb + 2):
        if step < rb:
            scores(step, step % 2)
        if 1 <= step <= rb:
            softmax(step - 1, (step - 1) % 2)
        if step >= 2:
            weighted_values(step - 2, (step - 2) % 2)


def _attention(q, k, v, bias_rows, *, rb):
    b, s, _ = q.shape
    rows = s // GRID_W
    assert rows % rb == 0
    tq = rb * GRID_W
    qspec = pl.BlockSpec((1, tq, NA_WIDTH), lambda bi, i: (bi, i, 0))
    kvspec = pl.BlockSpec((1, s, NA_WIDTH), lambda bi, i: (bi, 0, 0))
    pair_rows = 2 * GRID_W
    span = min(rb + NA_ROWS, rows)
    return pl.pallas_call(
        functools.partial(_attn_kernel, rows=rows, rb=rb, span=span),
        grid=(b, rows // rb),
        in_specs=[qspec, kvspec, kvspec, _const_spec(bias_rows.shape)],
        out_specs=qspec,
        out_shape=jax.ShapeDtypeStruct((b, s, NA_WIDTH), _BF16),
        scratch_shapes=[pltpu.VMEM((HEAD_PAIRS, pair_rows, NA_KEYS), _F32)] * 2
        + [pltpu.VMEM((HEAD_PAIRS, pair_rows, NA_KEYS), _BF16)] * 2
        + [pltpu.VMEM((HEAD_PAIRS, pair_rows, LANES), _F32)] * 2
        + [pltpu.VMEM((HEAD_PAIRS, span * GRID_W, 2 * LANES), _BF16),
           pltpu.VMEM((HEAD_PAIRS, N_BIAS_TILES, pair_rows, LANES), _F32)],
        compiler_params=pltpu.CompilerParams(
            dimension_semantics=("arbitrary", "arbitrary"), vmem_limit_bytes=VMEM_LIMIT_BYTES),
        name="natten",
    )(q, k, v, bias_rows)


def _mix_mlp_kernel(x_ref, p_ref, pprev_ref, pnext_ref, gate_ref, att_ref,
                    wgrp_ref, pscale_ref, wpool_ref, wna_ref, wout_ref,
                    nmlp_ref, w1_ref, w2_ref, nfin_ref, out_ref, ext_ref, *, tm, sub, seq):
    s0 = (pl.program_id(0) * tm) % seq
    ext_ref[0:POOL_HALO, :] = jnp.where(s0 == 0, 0.0, pprev_ref[...])
    ext_ref[POOL_HALO:POOL_HALO + tm, :] = p_ref[...]
    ext_ref[POOL_HALO + tm:, :] = jnp.where(s0 + tm == seq, 0.0, pnext_ref[...])

    def window_mean(tot, r0, w):
        h = w // 2
        mean = tot * (1.0 / w)

        def cut_rows(lo):
            pos = ((s0 + r0 + lo).astype(_F32)
                   + lax.broadcasted_iota(jnp.int32, (POOL_HALO, POOL_GROUP_W), 0).astype(_F32))
            cnt = jnp.minimum(pos + float(h), float(seq)) - jnp.maximum(pos - float(h), 0.0)
            return tot[lo:lo + POOL_HALO] / cnt

        if r0 == 0:
            mean = jnp.concatenate([cut_rows(0), mean[POOL_HALO:]], axis=0)
        if r0 + sub == tm:
            mean = jnp.concatenate([mean[:sub - POOL_HALO], cut_rows(sub - POOL_HALO)], axis=0)
        return mean

    def mixer(r0):
        rows = slice(r0, r0 + sub)
        mixed = []
        for g, w in enumerate(POOL_WINDOWS):
            ls = slice(g * POOL_GROUP_W, (g + 1) * POOL_GROUP_W)
            h = w // 2
            run = ext_ref[POOL_HALO + r0 - h:POOL_HALO + r0 + sub + h, ls]
            width = 1
            while 2 * width < w:
                n = run.shape[0] - width
                run = run[:n] + run[width:]
                width *= 2
            tot = run[:sub] + run[width:width + sub]
            pooled = window_mean(tot, r0, w) - p_ref[rows, ls]
            mixed.append(jnp.dot(pooled.astype(_BF16), wgrp_ref[g], preferred_element_type=_F32))
        mixed = jnp.concatenate(mixed, axis=1) * pscale_ref[...]
        pool_out = jnp.dot(mixed.astype(_BF16), wpool_ref[...], preferred_element_type=_F32)
        na_out = jnp.dot(att_ref[rows, :], wna_ref[...], preferred_element_type=_F32)
        merged = gate_ref[rows, :D_MODEL] * pool_out + gate_ref[rows, D_MODEL:] * na_out
        return x_ref[rows, :] + jnp.dot(merged.astype(_BF16), wout_ref[...], preferred_element_type=_F32)

    def mlp(r0, x1):
        xn = _rms(x1, nmlp_ref[...]).astype(_BF16)
        y = x1
        for c in range(D_FF // FF_CHUNK):
            cs = slice(c * FF_CHUNK, (c + 1) * FF_CHUNK)
            hid = jnp.square(jnp.maximum(jnp.dot(xn, w1_ref[:, cs], preferred_element_type=_F32), 0.0))
            y = y + jnp.dot(hid.astype(_BF16), w2_ref[cs, :], preferred_element_type=_F32)
        out_ref[r0:r0 + sub, :] = _rms(y, nfin_ref[...])

    starts = range(0, tm, sub)
    x1s = [mixer(r0) for r0 in starts]
    for r0, x1 in zip(starts, x1s):
        mlp(r0, x1)


def _mix_mlp(x2d, p, gates, att, wgrp, pscale, wpool, wna, wout, nmlp, w1, w2, nfin, *, tm, seq):
    n = x2d.shape[0]
    assert tm % MLP_SUB_TILE == 0 and seq % tm == 0
    hb = tm // POOL_HALO
    n_hb = n // POOL_HALO
    row = lambda w: pl.BlockSpec((tm, w), lambda i: (i, 0))
    prev = pl.BlockSpec((POOL_HALO, POOL_WIDTH), lambda i: (jnp.maximum(i * hb - 1, 0), 0))
    nxt = pl.BlockSpec((POOL_HALO, POOL_WIDTH), lambda i: (jnp.minimum((i + 1) * hb, n_hb - 1), 0))
    return pl.pallas_call(
        functools.partial(_mix_mlp_kernel, tm=tm, sub=MLP_SUB_TILE, seq=seq),
        grid=(n // tm,),
        in_specs=[row(D_MODEL), row(POOL_WIDTH), prev, nxt, row(GATE_WIDTH), row(NA_WIDTH),
                  _const_spec(wgrp.shape), _const_spec(pscale.shape), _const_spec(wpool.shape),
                  _const_spec(wna.shape), _const_spec(wout.shape), _const_spec(nmlp.shape),
                  _const_spec(w1.shape), _const_spec(w2.shape), _const_spec(nfin.shape)],
        out_specs=row(D_MODEL),
        out_shape=jax.ShapeDtypeStruct((n, D_MODEL), _F32),
        scratch_shapes=[pltpu.VMEM((tm + 2 * POOL_HALO, POOL_WIDTH), _F32)],
        compiler_params=pltpu.CompilerParams(
            dimension_semantics=("arbitrary",), vmem_limit_bytes=VMEM_LIMIT_BYTES),
        name="mix_mlp",
    )(x2d, p, p, p, gates, att, wgrp, pscale, wpool, wna, wout, nmlp, w1, w2, nfin)


def _layer(x, wts, bias_rows, *, tm_in, tm_mlp, rb):
    b, s, d = x.shape
    x2d = x.reshape(b * s, d)
    p, q, k, v, gates = _inproj(x2d, wts["norm_mix"], wts["w_in"], wts["b_gate"], tm=tm_in)
    shp = (b, s, NA_WIDTH)
    att = _attention(q.reshape(shp), k.reshape(shp), v.reshape(shp), bias_rows, rb=rb)
    y = _mix_mlp(x2d, p, gates, att.reshape(b * s, NA_WIDTH), wts["w_pool_grp"], wts["pool_scale"],
                 wts["w_pool_proj"], wts["w_na_proj"], wts["w_out"], wts["norm_mlp"],
                 wts["w_ff1"], wts["w_ff2"], wts["norm_final"], tm=tm_mlp, seq=s)
    return y.reshape(b, s, d)


def kernel(x_prompt, x_sample, norm_mix, w_in, b_gate, w_pool_grp, pool_scale, w_pool_proj, rpb,
           w_na_proj, w_out, norm_mlp, w_ff1, w_ff2, norm_final):
    depth = norm_mix.shape[0]
    assert depth == 1
    l = 0
    wts = {
        "norm_mix": norm_mix[l].reshape(1, D_MODEL),
        "w_in": w_in[l].astype(_BF16),
        "b_gate": b_gate[l].reshape(1, GATE_WIDTH),
        "w_pool_grp": w_pool_grp[l].astype(_BF16),
        "pool_scale": pool_scale[l].reshape(1, POOL_WIDTH),
        "w_pool_proj": w_pool_proj[l].astype(_BF16),
        "w_na_proj": w_na_proj[l].astype(_BF16),
        "w_out": w_out[l].astype(_BF16),
        "norm_mlp": norm_mlp[l].reshape(1, D_MODEL),
        "w_ff1": w_ff1[l].astype(_BF16),
        "w_ff2": w_ff2[l].astype(_BF16),
        "norm_final": norm_final.reshape(1, D_MODEL),
    }
    bias_rows = _bias_rows(rpb[l])
    run = functools.partial(_layer, wts=wts, bias_rows=bias_rows, tm_in=1024, tm_mlp=512, rb=32)
    return (run(x_prompt), run(x_sample))
```

```python
import functools

import jax
import jax.numpy as jnp
from jax import lax
from jax.experimental import pallas as pl
from jax.experimental.pallas import tpu as pltpu

D_MODEL = 1024
GRID_W = 64
POOL_WIDTH = 512
POOL_GROUPS = 4
POOL_GROUP_W = POOL_WIDTH // POOL_GROUPS
POOL_WINDOWS = (2, 4, 8, 16)
POOL_HALO = max(POOL_WINDOWS) // 2
NA_HEADS = 8
NA_HEAD_DIM = 64
NA_WIDTH = NA_HEADS * NA_HEAD_DIM
NA_ROWS = 8
NA_COLS = 16
NA_KEYS = NA_ROWS * GRID_W
HEAD_PAIRS = NA_HEADS // 2
ALL_HEAD_PAIRS = tuple(range(HEAD_PAIRS))
N_ROW_OFFSETS = 2 * NA_ROWS - 1
N_BIAS_TILES = N_ROW_OFFSETS - 1
D_FF = 4 * D_MODEL
FF_CHUNK = 1024
MLP_SUB_TILE = 256
GATE_WIDTH = 2 * D_MODEL
IN_WIDTH = POOL_WIDTH + 3 * NA_WIDTH + GATE_WIDTH
RMS_EPS = 1e-6
NEG_INF = -1e30
LOG2_E = 1.4426950408889634
QK_SCALE = NA_HEAD_DIM ** -0.5 * LOG2_E

LANES = 128
VMEM_LIMIT_BYTES = 60 * 1024 * 1024

_BF16 = jnp.bfloat16
_F32 = jnp.float32


def _rms(x, g):
    var = jnp.mean(x * x, axis=-1, keepdims=True)
    return x * lax.rsqrt(var + RMS_EPS) * g


def _const_spec(shape):
    zeros = (0,) * len(shape)
    return pl.BlockSpec(shape, lambda *_: zeros, pipeline_mode=pl.Buffered(1))


def _inproj_kernel(x_ref, g_ref, w_ref, b_ref, p_ref, q_ref, k_ref, v_ref, gate_ref):
    xn = _rms(x_ref[...], g_ref[...]).astype(_BF16)

    def proj(lo, width):
        return jnp.dot(xn, w_ref[:, lo:lo + width], preferred_element_type=_F32)

    g0 = POOL_WIDTH + 3 * NA_WIDTH
    for c in range(GATE_WIDTH // 512):
        sl = slice(512 * c, 512 * (c + 1))
        z = proj(g0 + 512 * c, 512) + b_ref[:, sl]
        gate_ref[:, sl] = (0.5 * jnp.tanh(0.5 * z) + 0.5).astype(_BF16)
    p_ref[...] = proj(0, POOL_WIDTH)
    q_ref[...] = (proj(POOL_WIDTH, NA_WIDTH) * QK_SCALE).astype(_BF16)
    k_ref[...] = proj(POOL_WIDTH + NA_WIDTH, NA_WIDTH).astype(_BF16)
    v_ref[...] = proj(POOL_WIDTH + 2 * NA_WIDTH, NA_WIDTH).astype(_BF16)


def _inproj(x2d, norm_g, w_in, b_gate, *, tm):
    n = x2d.shape[0]
    row = lambda w: pl.BlockSpec((tm, w), lambda i: (i, 0))
    return pl.pallas_call(
        _inproj_kernel,
        grid=(n // tm,),
        in_specs=[row(D_MODEL), _const_spec((1, D_MODEL)), _const_spec((D_MODEL, IN_WIDTH)),
                  _const_spec((1, GATE_WIDTH))],
        out_specs=[row(POOL_WIDTH), row(NA_WIDTH), row(NA_WIDTH), row(NA_WIDTH), row(GATE_WIDTH)],
        out_shape=[jax.ShapeDtypeStruct((n, POOL_WIDTH), _F32),
                   jax.ShapeDtypeStruct((n, NA_WIDTH), _BF16),
                   jax.ShapeDtypeStruct((n, NA_WIDTH), _BF16),
                   jax.ShapeDtypeStruct((n, NA_WIDTH), _BF16),
                   jax.ShapeDtypeStruct((n, GATE_WIDTH), _BF16)],
        compiler_params=pltpu.CompilerParams(
            dimension_semantics=("arbitrary",), vmem_limit_bytes=VMEM_LIMIT_BYTES),
        name="inproj",
    )(x2d, norm_g, w_in, b_gate)


def _bias_rows(rpb):
    lo = GRID_W - NA_COLS
    ext = jnp.pad(rpb.astype(_F32) * LOG2_E, ((0, 0), (0, 0), (lo, LANES - lo - (2 * NA_COLS - 1))), mode="edge")
    return ext.reshape(NA_HEADS * N_ROW_OFFSETS, LANES)


def _build_bias_tiles(rext_ref, tb_ref):
    qc = lax.broadcasted_iota(jnp.int32, (GRID_W, LANES), 0)
    lane = lax.broadcasted_iota(jnp.int32, (GRID_W, LANES), 1)
    kc = lane & (GRID_W - 1)
    cs = jnp.clip(qc - NA_COLS // 2, 0, GRID_W - NA_COLS)
    valid = (kc >= cs) & (kc < cs + NA_COLS)
    left = lane < GRID_W
    for h in range(NA_HEADS):
        rows = slice((h % 2) * GRID_W, (h % 2 + 1) * GRID_W)

        def shifted(j, first_lane):
            row = jnp.broadcast_to(rext_ref[pl.ds(h * N_ROW_OFFSETS + j, 1), :], (GRID_W, LANES))
            return pltpu.roll(row, (first_lane + GRID_W + 1) % LANES, 1, stride=1, stride_axis=0)

        for j in range(N_BIAS_TILES):
            tile = jnp.where(left, shifted(j, 0), shifted(j + 1, GRID_W))
            tb_ref[h // 2, j, rows, :] = jnp.where(valid, tile, NEG_INF)


def _attn_kernel(q_ref, k_ref, v_ref, rext_ref, o_ref, s0_ref, s1_ref, p0_ref, p1_ref, m0_ref, m1_ref,
                 vones_ref, tb_ref, *, rows, rb, span):
    blk = pl.program_id(1)
    even_head = lax.broadcasted_iota(jnp.int32, (GRID_W, LANES), 1) < NA_HEAD_DIM
    s_slots, p_slots, m_slots = (s0_ref, s1_ref), (p0_ref, p1_ref), (m0_ref, m1_ref)
    lane_tiles = [slice(m * LANES, (m + 1) * LANES) for m in range(NA_KEYS // LANES)]
    base_row = jnp.clip(blk * rb - NA_ROWS // 2, 0, rows - span)

    @pl.when((pl.program_id(0) == 0) & (blk == 0))
    def _():
        vones_ref[:, :, LANES:] = jnp.ones((HEAD_PAIRS, span * GRID_W, LANES), _BF16)
        _build_bias_tiles(rext_ref, tb_ref)

    v0 = pl.multiple_of(base_row * GRID_W, GRID_W)
    for hp in ALL_HEAD_PAIRS:
        vones_ref[hp, :, :LANES] = v_ref[0, pl.ds(v0, span * GRID_W), hp * LANES:(hp + 1) * LANES]

    def key_rows(j):
        r = blk * rb + j
        rs = jnp.clip(r - NA_ROWS // 2, 0, rows - NA_ROWS)
        return rs, r - rs

    def scores(j, slot, hps=ALL_HEAD_PAIRS):
        rs, delta = key_rows(j)
        k0 = pl.multiple_of(rs * GRID_W, GRID_W)
        q0 = pl.multiple_of(j * GRID_W, GRID_W)
        for hp in hps:
            ls = slice(hp * LANES, (hp + 1) * LANES)
            qp = q_ref[0, pl.ds(q0, GRID_W), ls]
            zero = jnp.zeros_like(qp)
            q2 = jnp.concatenate([jnp.where(even_head, qp, zero), jnp.where(even_head, zero, qp)], axis=0)
            kp = k_ref[0, pl.ds(k0, NA_KEYS), ls]
            qk = lax.dot_general(q2, kp, (((1,), (1,)), ((), ())), preferred_element_type=_F32)
            s = [qk[:, lt] + tb_ref[hp, 2 * m - delta + (NA_ROWS - 1)] for m, lt in enumerate(lane_tiles)]
            for sm, lt in zip(s, lane_tiles):
                s_slots[slot][hp, :, lt] = sm
            mx = jnp.max(functools.reduce(jnp.maximum, s), axis=-1, keepdims=True)
            m_slots[slot][hp] = jnp.broadcast_to(mx, (2 * GRID_W, LANES))

    def softmax(j, slot, hps=ALL_HEAD_PAIRS):
        for hp in hps:
            mx = m_slots[slot][hp]
            for lt in lane_tiles:
                p_slots[slot][hp, :, lt] = jnp.exp2(s_slots[slot][hp, :, lt] - mx).astype(_BF16)

    def weighted_values(j, slot, hps=ALL_HEAD_PAIRS):
        rs, _ = key_rows(j)
        k0 = pl.multiple_of((rs - base_row) * GRID_W, GRID_W)
        q0 = pl.multiple_of(j * GRID_W, GRID_W)
        for hp in hps:
            ls = slice(hp * LANES, (hp + 1) * LANES)
            vo = vones_ref[hp, pl.ds(k0, NA_KEYS), :]
            o2l = jnp.dot(p_slots[slot][hp], vo, preferred_element_type=_F32)
            o2 = o2l[:, :LANES] / o2l[:, LANES:]
            out = jnp.where(even_head, o2[:GRID_W], o2[GRID_W:])
            o_ref[0, pl.ds(q0, GRID_W), ls] = out.astype(_BF16)

    for step in range(rb + 2):
        if step < rb:
            scores(step, step % 2)
        if 1 <= step <= rb:
            softmax(step - 1, (step - 1) % 2)
        if step >= 2:
            weighted_values(step - 2, (step - 2) % 2)


def _attention(q, k, v, bias_rows, *, rb):
    b, s, _ = q.shape
    rows = s // GRID_W
    assert rows % rb == 0
    tq = rb * GRID_W
    qspec = pl.BlockSpec((1, tq, NA_WIDTH), lambda bi, i: (bi, i, 0))
    kvspec = pl.BlockSpec((1, s, NA_WIDTH), lambda bi, i: (bi, 0, 0))
    pair_rows = 2 * GRID_W
    span = min(rb + NA_ROWS, rows)
    return pl.pallas_call(
        functools.partial(_attn_kernel, rows=rows, rb=rb, span=span),
        grid=(b, rows // rb),
        in_specs=[qspec, kvspec, kvspec, _const_spec(bias_rows.shape)],
        out_specs=qspec,
        out_shape=jax.ShapeDtypeStruct((b, s, NA_WIDTH), _BF16),
        scratch_shapes=[pltpu.VMEM((HEAD_PAIRS, pair_rows, NA_KEYS), _F32)] * 2
        + [pltpu.VMEM((HEAD_PAIRS, pair_rows, NA_KEYS), _BF16)] * 2
        + [pltpu.VMEM((HEAD_PAIRS, pair_rows, LANES), _F32)] * 2
        + [pltpu.VMEM((HEAD_PAIRS, span * GRID_W, 2 * LANES), _BF16),
           pltpu.VMEM((HEAD_PAIRS, N_BIAS_TILES, pair_rows, LANES), _F32)],
        compiler_params=pltpu.CompilerParams(
            dimension_semantics=("arbitrary", "arbitrary"), vmem_limit_bytes=VMEM_LIMIT_BYTES),
        name="natten",
    )(q, k, v, bias_rows)


def _mix_mlp_kernel(x_ref, p_ref, pprev_ref, pnext_ref, gate_ref, att_ref,
                    wgrp_ref, pscale_ref, wpool_ref, wna_ref, wout_ref,
                    nmlp_ref, w1_ref, w2_ref, nfin_ref, out_ref, ext_ref, *, tm, sub, seq):
    s0 = (pl.program_id(0) * tm) % seq
    ext_ref[0:POOL_HALO, :] = jnp.where(s0 == 0, 0.0, pprev_ref[...])
    ext_ref[POOL_HALO:POOL_HALO + tm, :] = p_ref[...]
    ext_ref[POOL_HALO + tm:, :] = jnp.where(s0 + tm == seq, 0.0, pnext_ref[...])

    def window_mean(tot, r0, w):
        h = w // 2
        mean = tot * (1.0 / w)

        def cut_rows(lo):
            pos = ((s0 + r0 + lo).astype(_F32)
                   + lax.broadcasted_iota(jnp.int32, (POOL_HALO, POOL_GROUP_W), 0).astype(_F32))
            cnt = jnp.minimum(pos + float(h), float(seq)) - jnp.maximum(pos - float(h), 0.0)
            return tot[lo:lo + POOL_HALO] / cnt

        if r0 == 0:
            mean = jnp.concatenate([cut_rows(0), mean[POOL_HALO:]], axis=0)
        if r0 + sub == tm:
            mean = jnp.concatenate([mean[:sub - POOL_HALO], cut_rows(sub - POOL_HALO)], axis=0)
        return mean

    def mixer(r0):
        rows = slice(r0, r0 + sub)
        mixed = []
        for g, w in enumerate(POOL_WINDOWS):
            ls = slice(g * POOL_GROUP_W, (g + 1) * POOL_GROUP_W)
            h = w // 2
            run = ext_ref[POOL_HALO + r0 - h:POOL_HALO + r0 + sub + h, ls]
            width = 1
            while 2 * width < w:
                n = run.shape[0] - width
                run = run[:n] + run[width:]
                width *= 2
            tot = run[:sub] + run[width:width + sub]
            pooled = window_mean(tot, r0, w) - p_ref[rows, ls]
            mixed.append(jnp.dot(pooled.astype(_BF16), wgrp_ref[g], preferred_element_type=_F32))
        mixed = jnp.concatenate(mixed, axis=1) * pscale_ref[...]
        pool_out = jnp.dot(mixed.astype(_BF16), wpool_ref[...], preferred_element_type=_F32)
        na_out = jnp.dot(att_ref[rows, :], wna_ref[...], preferred_element_type=_F32)
        merged = gate_ref[rows, :D_MODEL] * pool_out + gate_ref[rows, D_MODEL:] * na_out
        return x_ref[rows, :] + jnp.dot(merged.astype(_BF16), wout_ref[...], preferred_element_type=_F32)

    def mlp(r0, x1):
        xn = _rms(x1, nmlp_ref[...]).astype(_BF16)
        y = x1
        for c in range(D_FF // FF_CHUNK):
            cs = slice(c * FF_CHUNK, (c + 1) * FF_CHUNK)
            hid = jnp.square(jnp.maximum(jnp.dot(xn, w1_ref[:, cs], preferred_element_type=_F32), 0.0))
            y = y + jnp.dot(hid.astype(_BF16), w2_ref[cs, :], preferred_element_type=_F32)
        out_ref[r0:r0 + sub, :] = _rms(y, nfin_ref[...])

    starts = list(range(0, tm, sub))
    x1 = mixer(starts[0])
    for r0, r1 in zip(starts, starts[1:] + [None]):
        x1_next = None if r1 is None else mixer(r1)
        mlp(r0, x1)
        x1 = x1_next


def _mix_mlp(x2d, p, gates, att, wgrp, pscale, wpool, wna, wout, nmlp, w1, w2, nfin, *, tm, seq):
    n = x2d.shape[0]
    assert tm % MLP_SUB_TILE == 0 and seq % tm == 0
    hb = tm // POOL_HALO
    n_hb = n // POOL_HALO
    row = lambda w: pl.BlockSpec((tm, w), lambda i: (i, 0))
    prev = pl.BlockSpec((POOL_HALO, POOL_WIDTH), lambda i: (jnp.maximum(i * hb - 1, 0), 0))
    nxt = pl.BlockSpec((POOL_HALO, POOL_WIDTH), lambda i: (jnp.minimum((i + 1) * hb, n_hb - 1), 0))
    return pl.pallas_call(
        functools.partial(_mix_mlp_kernel, tm=tm, sub=MLP_SUB_TILE, seq=seq),
        grid=(n // tm,),
        in_specs=[row(D_MODEL), row(POOL_WIDTH), prev, nxt, row(GATE_WIDTH), row(NA_WIDTH),
                  _const_spec(wgrp.shape), _const_spec(pscale.shape), _const_spec(wpool.shape),
                  _const_spec(wna.shape), _const_spec(wout.shape), _const_spec(nmlp.shape),
                  _const_spec(w1.shape), _const_spec(w2.shape), _const_spec(nfin.shape)],
        out_specs=row(D_MODEL),
        out_shape=jax.ShapeDtypeStruct((n, D_MODEL), _F32),
        scratch_shapes=[pltpu.VMEM((tm + 2 * POOL_HALO, POOL_WIDTH), _F32)],
        compiler_params=pltpu.CompilerParams(
            dimension_semantics=("arbitrary",), vmem_limit_bytes=VMEM_LIMIT_BYTES),
        name="mix_mlp",
    )(x2d, p, p, p, gates, att, wgrp, pscale, wpool, wna, wout, nmlp, w1, w2, nfin)


def _layer(x, wts, bias_rows, *, tm_in, tm_mlp, rb):
    b, s, d = x.shape
    x2d = x.reshape(b * s, d)
    p, q, k, v, gates = _inproj(x2d, wts["norm_mix"], wts["w_in"], wts["b_gate"], tm=tm_in)
    shp = (b, s, NA_WIDTH)
    att = _attention(q.reshape(shp), k.reshape(shp), v.reshape(shp), bias_rows, rb=rb)
    y = _mix_mlp(x2d, p, gates, att.reshape(b * s, NA_WIDTH), wts["w_pool_grp"], wts["pool_scale"],
                 wts["w_pool_proj"], wts["w_na_proj"], wts["w_out"], wts["norm_mlp"],
                 wts["w_ff1"], wts["w_ff2"], wts["norm_final"], tm=tm_mlp, seq=s)
    return y.reshape(b, s, d)


def kernel(x_prompt, x_sample, norm_mix, w_in, b_gate, w_pool_grp, pool_scale, w_pool_proj, rpb,
           w_na_proj, w_out, norm_mlp, w_ff1, w_ff2, norm_final):
    depth = norm_mix.shape[0]
    assert depth == 1
    l = 0
    wts = {
        "norm_mix": norm_mix[l].reshape(1, D_MODEL),
        "w_in": w_in[l].astype(_BF16),
        "b_gate": b_gate[l].reshape(1, GATE_WIDTH),
        "w_pool_grp": w_pool_grp[l].astype(_BF16),
        "pool_scale": pool_scale[l].reshape(1, POOL_WIDTH),
        "w_pool_proj": w_pool_proj[l].astype(_BF16),
        "w_na_proj": w_na_proj[l].astype(_BF16),
        "w_out": w_out[l].astype(_BF16),
        "norm_mlp": norm_mlp[l].reshape(1, D_MODEL),
        "w_ff1": w_ff1[l].astype(_BF16),
        "w_ff2": w_ff2[l].astype(_BF16),
        "norm_final": norm_final.reshape(1, D_MODEL),
    }
    bias_rows = _bias_rows(rpb[l])
    run = functools.partial(_layer, wts=wts, bias_rows=bias_rows, tm_in=1024, tm_mlp=1024, rb=32)
    return (run(x_prompt), run(x_sample))
```

```python
import functools

import jax
import jax.numpy as jnp
from jax import lax
from jax.experimental import pallas as pl
from jax.experimental.pallas import tpu as pltpu

D_MODEL = 1024
GRID_W = 64
POOL_WIDTH = 512
POOL_GROUPS = 4
POOL_GROUP_W = POOL_WIDTH // POOL_GROUPS
POOL_WINDOWS = (2, 4, 8, 16)
POOL_HALO = max(POOL_WINDOWS) // 2
NA_HEADS = 8
NA_HEAD_DIM = 64
NA_WIDTH = NA_HEADS * NA_HEAD_DIM
NA_ROWS = 8
NA_COLS = 16
NA_KEYS = NA_ROWS * GRID_W
HEAD_PAIRS = NA_HEADS // 2
ALL_HEAD_PAIRS = tuple(range(HEAD_PAIRS))
N_ROW_OFFSETS = 2 * NA_ROWS - 1
N_BIAS_TILES = N_ROW_OFFSETS - 1
ATT_UNROLL = 10
D_FF = 4 * D_MODEL
FF_CHUNK = 1024
MLP_SUB_TILE = 256
GATE_WIDTH = 2 * D_MODEL
IN_WIDTH = POOL_WIDTH + 3 * NA_WIDTH + GATE_WIDTH
RMS_EPS = 1e-6
NEG_INF = -1e30
LOG2_E = 1.4426950408889634
QK_SCALE = NA_HEAD_DIM ** -0.5 * LOG2_E

LANES = 128
VMEM_LIMIT_BYTES = 56 * 1024 * 1024

_BF16 = jnp.bfloat16
_F32 = jnp.float32


def _rms(x, g):
    var = jnp.mean(x * x, axis=-1, keepdims=True)
    return x * lax.rsqrt(var + RMS_EPS) * g


def _const_spec(shape):
    zeros = (0,) * len(shape)
    return pl.BlockSpec(shape, lambda *_: zeros, pipeline_mode=pl.Buffered(1))


def _inproj_kernel(x_ref, g_ref, w_ref, b_ref, p_ref, q_ref, k_ref, v_ref, gate_ref):
    xn = _rms(x_ref[...], g_ref[...]).astype(_BF16)

    def proj(lo, width):
        return jnp.dot(xn, w_ref[:, lo:lo + width], preferred_element_type=_F32)

    g0 = POOL_WIDTH + 3 * NA_WIDTH
    for c in range(GATE_WIDTH // 512):
        sl = slice(512 * c, 512 * (c + 1))
        z = proj(g0 + 512 * c, 512) + b_ref[:, sl]
        gate_ref[:, sl] = (0.5 * jnp.tanh(0.5 * z) + 0.5).astype(_BF16)
    p_ref[...] = proj(0, POOL_WIDTH)
    q_ref[...] = (proj(POOL_WIDTH, NA_WIDTH) * QK_SCALE).astype(_BF16)
    k_ref[...] = proj(POOL_WIDTH + NA_WIDTH, NA_WIDTH).astype(_BF16)
    v_ref[...] = proj(POOL_WIDTH + 2 * NA_WIDTH, NA_WIDTH).astype(_BF16)


def _inproj(x2d, norm_g, w_in, b_gate, *, tm):
    n = x2d.shape[0]
    row = lambda w: pl.BlockSpec((tm, w), lambda i: (i, 0))
    return pl.pallas_call(
        _inproj_kernel,
        grid=(n // tm,),
        in_specs=[row(D_MODEL), _const_spec((1, D_MODEL)), _const_spec((D_MODEL, IN_WIDTH)),
                  _const_spec((1, GATE_WIDTH))],
        out_specs=[row(POOL_WIDTH), row(NA_WIDTH), row(NA_WIDTH), row(NA_WIDTH), row(GATE_WIDTH)],
        out_shape=[jax.ShapeDtypeStruct((n, POOL_WIDTH), _F32),
                   jax.ShapeDtypeStruct((n, NA_WIDTH), _BF16),
                   jax.ShapeDtypeStruct((n, NA_WIDTH), _BF16),
                   jax.ShapeDtypeStruct((n, NA_WIDTH), _BF16),
                   jax.ShapeDtypeStruct((n, GATE_WIDTH), _BF16)],
        compiler_params=pltpu.CompilerParams(
            dimension_semantics=("arbitrary",), vmem_limit_bytes=VMEM_LIMIT_BYTES),
        name="inproj",
    )(x2d, norm_g, w_in, b_gate)


def _bias_rows(rpb):
    lo = GRID_W - NA_COLS
    ext = jnp.pad(rpb.astype(_F32) * LOG2_E, ((0, 0), (0, 0), (lo, LANES - lo - (2 * NA_COLS - 1))), mode="edge")
    return ext.reshape(NA_HEADS * N_ROW_OFFSETS, LANES)


def _build_bias_tiles(rext_ref, tb_ref):
    qc = lax.broadcasted_iota(jnp.int32, (GRID_W, LANES), 0)
    lane = lax.broadcasted_iota(jnp.int32, (GRID_W, LANES), 1)
    kc = lane & (GRID_W - 1)
    cs = jnp.clip(qc - NA_COLS // 2, 0, GRID_W - NA_COLS)
    valid = (kc >= cs) & (kc < cs + NA_COLS)
    left = lane < GRID_W
    for h in range(NA_HEADS):
        rows = slice((h % 2) * GRID_W, (h % 2 + 1) * GRID_W)

        def shifted(j, first_lane):
            row = jnp.broadcast_to(rext_ref[pl.ds(h * N_ROW_OFFSETS + j, 1), :], (GRID_W, LANES))
            return pltpu.roll(row, (first_lane + GRID_W + 1) % LANES, 1, stride=1, stride_axis=0)

        for j in range(N_BIAS_TILES):
            tile = jnp.where(left, shifted(j, 0), shifted(j + 1, GRID_W))
            tb_ref[h // 2, j, rows, :] = jnp.where(valid, tile, NEG_INF)


def _attn_kernel(q_ref, k_ref, v_ref, rext_ref, o_ref, s0_ref, s1_ref, p0_ref, p1_ref, m0_ref, m1_ref,
                 vones_ref, tb_ref, *, rows, rb, span):
    blk = pl.program_id(1)
    even_head = lax.broadcasted_iota(jnp.int32, (GRID_W, LANES), 1) < NA_HEAD_DIM
    s_slots, p_slots, m_slots = (s0_ref, s1_ref), (p0_ref, p1_ref), (m0_ref, m1_ref)
    lane_tiles = [slice(m * LANES, (m + 1) * LANES) for m in range(NA_KEYS // LANES)]
    base_row = jnp.clip(blk * rb - NA_ROWS // 2, 0, rows - span)

    @pl.when((pl.program_id(0) == 0) & (blk == 0))
    def _():
        vones_ref[:, :, LANES:] = jnp.ones((HEAD_PAIRS, span * GRID_W, LANES), _BF16)
        _build_bias_tiles(rext_ref, tb_ref)

    v0 = pl.multiple_of(base_row * GRID_W, GRID_W)
    for hp in ALL_HEAD_PAIRS:
        vones_ref[hp, :, :LANES] = v_ref[0, pl.ds(v0, span * GRID_W), hp * LANES:(hp + 1) * LANES]

    def key_rows(j):
        r = blk * rb + j
        rs = jnp.clip(r - NA_ROWS // 2, 0, rows - NA_ROWS)
        return rs, r - rs

    def scores(j, slot, hps=ALL_HEAD_PAIRS):
        rs, delta = key_rows(j)
        k0 = pl.multiple_of(rs * GRID_W, GRID_W)
        q0 = pl.multiple_of(j * GRID_W, GRID_W)
        for hp in hps:
            ls = slice(hp * LANES, (hp + 1) * LANES)
            qp = q_ref[0, pl.ds(q0, GRID_W), ls]
            zero = jnp.zeros_like(qp)
            q2 = jnp.concatenate([jnp.where(even_head, qp, zero), jnp.where(even_head, zero, qp)], axis=0)
            kp = k_ref[0, pl.ds(k0, NA_KEYS), ls]
            qk = lax.dot_general(q2, kp, (((1,), (1,)), ((), ())), preferred_element_type=_F32)
            s = [qk[:, lt] + tb_ref[hp, 2 * m - delta + (NA_ROWS - 1)] for m, lt in enumerate(lane_tiles)]
            for sm, lt in zip(s, lane_tiles):
                s_slots[slot][hp, :, lt] = sm
            mx = jnp.max(functools.reduce(jnp.maximum, s), axis=-1, keepdims=True)
            m_slots[slot][hp] = jnp.broadcast_to(mx, (2 * GRID_W, LANES))

    def softmax(j, slot, hps=ALL_HEAD_PAIRS):
        for hp in hps:
            mx = m_slots[slot][hp]
            for lt in lane_tiles:
                p_slots[slot][hp, :, lt] = jnp.exp2(s_slots[slot][hp, :, lt] - mx).astype(_BF16)

    def weighted_values(j, slot, hps=ALL_HEAD_PAIRS):
        rs, _ = key_rows(j)
        k0 = pl.multiple_of((rs - base_row) * GRID_W, GRID_W)
        q0 = pl.multiple_of(j * GRID_W, GRID_W)
        for hp in hps:
            ls = slice(hp * LANES, (hp + 1) * LANES)
            vo = vones_ref[hp, pl.ds(k0, NA_KEYS), :]
            o2l = jnp.dot(p_slots[slot][hp], vo, preferred_element_type=_F32)
            o2 = o2l[:, :LANES] / o2l[:, LANES:]
            out = jnp.where(even_head, o2[:GRID_W], o2[GRID_W:])
            o_ref[0, pl.ds(q0, GRID_W), ls] = out.astype(_BF16)

    scores(0, 0)
    scores(1, 1)
    softmax(0, 0)

    def group(u, carry):
        for i in range(ATT_UNROLL):
            step = 2 + ATT_UNROLL * u + i
            scores(step, i % 2)
            softmax(step - 1, (i + 1) % 2)
            weighted_values(step - 2, i % 2)
        return carry

    lax.fori_loop(0, (rb - 2) // ATT_UNROLL, group, 0)
    softmax(rb - 1, (rb - 1) % 2)
    weighted_values(rb - 2, rb % 2)
    weighted_values(rb - 1, (rb - 1) % 2)


def _attention(q, k, v, bias_rows, *, rb):
    b, s, _ = q.shape
    rows = s // GRID_W
    assert rows % rb == 0
    tq = rb * GRID_W
    qspec = pl.BlockSpec((1, tq, NA_WIDTH), lambda bi, i: (bi, i, 0))
    kvspec = pl.BlockSpec((1, s, NA_WIDTH), lambda bi, i: (bi, 0, 0))
    pair_rows = 2 * GRID_W
    span = min(rb + NA_ROWS, rows)
    return pl.pallas_call(
        functools.partial(_attn_kernel, rows=rows, rb=rb, span=span),
        grid=(b, rows // rb),
        in_specs=[qspec, kvspec, kvspec, _const_spec(bias_rows.shape)],
        out_specs=qspec,
        out_shape=jax.ShapeDtypeStruct((b, s, NA_WIDTH), _BF16),
        scratch_shapes=[pltpu.VMEM((HEAD_PAIRS, pair_rows, NA_KEYS), _F32)] * 2
        + [pltpu.VMEM((HEAD_PAIRS, pair_rows, NA_KEYS), _BF16)] * 2
        + [pltpu.VMEM((HEAD_PAIRS, pair_rows, LANES), _F32)] * 2
        + [pltpu.VMEM((HEAD_PAIRS, span * GRID_W, 2 * LANES), _BF16),
           pltpu.VMEM((HEAD_PAIRS, N_BIAS_TILES, pair_rows, LANES), _F32)],
        compiler_params=pltpu.CompilerParams(
            dimension_semantics=("arbitrary", "arbitrary"), vmem_limit_bytes=VMEM_LIMIT_BYTES),
        name="natten",
    )(q, k, v, bias_rows)


def _mix_mlp_kernel(x_ref, p_ref, pprev_ref, pnext_ref, gate_ref, att_ref,
                    wgrp_ref, pscale_ref, wpool_ref, wna_ref, wout_ref,
                    nmlp_ref, w1_ref, w2_ref, nfin_ref, out_ref, ext_ref, *, tm, sub, seq):
    s0 = (pl.program_id(0) * tm) % seq
    ext_ref[0:POOL_HALO, :] = jnp.where(s0 == 0, 0.0, pprev_ref[...])
    ext_ref[POOL_HALO:POOL_HALO + tm, :] = p_ref[...]
    ext_ref[POOL_HALO + tm:, :] = jnp.where(s0 + tm == seq, 0.0, pnext_ref[...])

    def window_mean(tot, r0, w):
        h = w // 2
        mean = tot * (1.0 / w)

        def cut_rows(lo):
            pos = ((s0 + r0 + lo).astype(_F32)
                   + lax.broadcasted_iota(jnp.int32, (POOL_HALO, POOL_GROUP_W), 0).astype(_F32))
            cnt = jnp.minimum(pos + float(h), float(seq)) - jnp.maximum(pos - float(h), 0.0)
            return tot[lo:lo + POOL_HALO] / cnt

        if r0 == 0:
            mean = jnp.concatenate([cut_rows(0), mean[POOL_HALO:]], axis=0)
        if r0 + sub == tm:
            mean = jnp.concatenate([mean[:sub - POOL_HALO], cut_rows(sub - POOL_HALO)], axis=0)
        return mean

    def mixer(r0):
        rows = slice(r0, r0 + sub)
        mixed = []
        for g, w in enumerate(POOL_WINDOWS):
            ls = slice(g * POOL_GROUP_W, (g + 1) * POOL_GROUP_W)
            h = w // 2
            run = ext_ref[POOL_HALO + r0 - h:POOL_HALO + r0 + sub + h, ls]
            width = 1
            while 2 * width < w:
                n = run.shape[0] - width
                run = run[:n] + run[width:]
                width *= 2
            tot = run[:sub] + run[width:width + sub]
            pooled = window_mean(tot, r0, w) - p_ref[rows, ls]
            mixed.append(jnp.dot(pooled.astype(_BF16), wgrp_ref[g], preferred_element_type=_F32))
        mixed = jnp.concatenate(mixed, axis=1) * pscale_ref[...]
        pool_out = jnp.dot(mixed.astype(_BF16), wpool_ref[...], preferred_element_type=_F32)
        na_out = jnp.dot(att_ref[rows, :], wna_ref[...], preferred_element_type=_F32)
        merged = gate_ref[rows, :D_MODEL] * pool_out + gate_ref[rows, D_MODEL:] * na_out
        return x_ref[rows, :] + jnp.dot(merged.astype(_BF16), wout_ref[...], preferred_element_type=_F32)

    def mlp(r0, x1):
        xn = _rms(x1, nmlp_ref[...]).astype(_BF16)
        y = x1
        for c in range(D_FF // FF_CHUNK):
            cs = slice(c * FF_CHUNK, (c + 1) * FF_CHUNK)
            hid = jnp.square(jnp.maximum(jnp.dot(xn, w1_ref[:, cs], preferred_element_type=_F32), 0.0))
            y = y + jnp.dot(hid.astype(_BF16), w2_ref[cs, :], preferred_element_type=_F32)
        out_ref[r0:r0 + sub, :] = _rms(y, nfin_ref[...])

    starts = list(range(0, tm, sub))
    x1 = mixer(starts[0])
    for r0, r1 in zip(starts, starts[1:] + [None]):
        x1_next = None if r1 is None else mixer(r1)
        mlp(r0, x1)
        x1 = x1_next


def _mix_mlp(x2d, p, gates, att, wgrp, pscale, wpool, wna, wout, nmlp, w1, w2, nfin, *, tm, seq):
    n = x2d.shape[0]
    assert tm % MLP_SUB_TILE == 0 and seq % tm == 0
    hb = tm // POOL_HALO
    n_hb = n // POOL_HALO
    row = lambda w: pl.BlockSpec((tm, w), lambda i: (i, 0))
    prev = pl.BlockSpec((POOL_HALO, POOL_WIDTH), lambda i: (jnp.maximum(i * hb - 1, 0), 0))
    nxt = pl.BlockSpec((POOL_HALO, POOL_WIDTH), lambda i: (jnp.minimum((i + 1) * hb, n_hb - 1), 0))
    return pl.pallas_call(
        functools.partial(_mix_mlp_kernel, tm=tm, sub=MLP_SUB_TILE, seq=seq),
        grid=(n // tm,),
        in_specs=[row(D_MODEL), row(POOL_WIDTH), prev, nxt, row(GATE_WIDTH), row(NA_WIDTH),
                  _const_spec(wgrp.shape), _const_spec(pscale.shape), _const_spec(wpool.shape),
                  _const_spec(wna.shape), _const_spec(wout.shape), _const_spec(nmlp.shape),
                  _const_spec(w1.shape), _const_spec(w2.shape), _const_spec(nfin.shape)],
        out_specs=row(D_MODEL),
        out_shape=jax.ShapeDtypeStruct((n, D_MODEL), _F32),
        scratch_shapes=[pltpu.VMEM((tm + 2 * POOL_HALO, POOL_WIDTH), _F32)],
        compiler_params=pltpu.CompilerParams(
            dimension_semantics=("arbitrary",), vmem_limit_bytes=VMEM_LIMIT_BYTES),
        name="mix_mlp",
    )(x2d, p, p, p, gates, att, wgrp, pscale, wpool, wna, wout, nmlp, w1, w2, nfin)


def _layer(x, wts, bias_rows, *, tm_in, tm_mlp, rb):
    b, s, d = x.shape
    x2d = x.reshape(b * s, d)
    p, q, k, v, gates = _inproj(x2d, wts["norm_mix"], wts["w_in"], wts["b_gate"], tm=tm_in)
    shp = (b, s, NA_WIDTH)
    att = _attention(q.reshape(shp), k.reshape(shp), v.reshape(shp), bias_rows, rb=rb)
    y = _mix_mlp(x2d, p, gates, att.reshape(b * s, NA_WIDTH), wts["w_pool_grp"], wts["pool_scale"],
                 wts["w_pool_proj"], wts["w_na_proj"], wts["w_out"], wts["norm_mlp"],
                 wts["w_ff1"], wts["w_ff2"], wts["norm_final"], tm=tm_mlp, seq=s)
    return y.reshape(b, s, d)


def kernel(x_prompt, x_sample, norm_mix, w_in, b_gate, w_pool_grp, pool_scale, w_pool_proj, rpb,
           w_na_proj, w_out, norm_mlp, w_ff1, w_ff2, norm_final):
    depth = norm_mix.shape[0]
    assert depth == 1
    l = 0
    wts = {
        "norm_mix": norm_mix[l].reshape(1, D_MODEL),
        "w_in": w_in[l].astype(_BF16),
        "b_gate": b_gate[l].reshape(1, GATE_WIDTH),
        "w_pool_grp": w_pool_grp[l].astype(_BF16),
        "pool_scale": pool_scale[l].reshape(1, POOL_WIDTH),
        "w_pool_proj": w_pool_proj[l].astype(_BF16),
        "w_na_proj": w_na_proj[l].astype(_BF16),
        "w_out": w_out[l].astype(_BF16),
        "norm_mlp": norm_mlp[l].reshape(1, D_MODEL),
        "w_ff1": w_ff1[l].astype(_BF16),
        "w_ff2": w_ff2[l].astype(_BF16),
        "norm_final": norm_final.reshape(1, D_MODEL),
    }
    bias_rows = _bias_rows(rpb[l])
    run = functools.partial(_layer, wts=wts, bias_rows=bias_rows, tm_in=1024, tm_mlp=512, rb=32)
    return (run(x_prompt), run(x_sample))
```

```python
import functools

import jax
import jax.numpy as jnp
from jax import lax
from jax.experimental import pallas as pl
from jax.experimental.pallas import tpu as pltpu

D_MODEL = 1024
GRID_W = 64
POOL_WIDTH = 512
POOL_GROUPS = 4
POOL_GROUP_W = POOL_WIDTH // POOL_GROUPS
POOL_WINDOWS = (2, 4, 8, 16)
POOL_HALO = max(POOL_WINDOWS) // 2
NA_HEADS = 8
NA_HEAD_DIM = 64
NA_WIDTH = NA_HEADS * NA_HEAD_DIM
NA_ROWS = 8
NA_COLS = 16
NA_KEYS = NA_ROWS * GRID_W
HEAD_PAIRS = NA_HEADS // 2
ALL_HEAD_PAIRS = tuple(range(HEAD_PAIRS))
N_ROW_OFFSETS = 2 * NA_ROWS - 1
N_BIAS_TILES = N_ROW_OFFSETS - 1
D_FF = 4 * D_MODEL
FF_CHUNK = 1024
MLP_SUB_TILE = 256
GATE_WIDTH = 2 * D_MODEL
IN_WIDTH = POOL_WIDTH + 3 * NA_WIDTH + GATE_WIDTH
RMS_EPS = 1e-6
NEG_INF = -1e30
LOG2_E = 1.4426950408889634
QK_SCALE = NA_HEAD_DIM ** -0.5 * LOG2_E

LANES = 128
VMEM_LIMIT_BYTES = 56 * 1024 * 1024
ATT_VMEM_LIMIT_BYTES = 60 * 1024 * 1024

_BF16 = jnp.bfloat16
_F32 = jnp.float32


def _rms(x, g):
    var = jnp.mean(x * x, axis=-1, keepdims=True)
    return x * lax.rsqrt(var + RMS_EPS) * g


def _const_spec(shape):
    zeros = (0,) * len(shape)
    return pl.BlockSpec(shape, lambda *_: zeros, pipeline_mode=pl.Buffered(1))


def _inproj_kernel(x_ref, g_ref, w_ref, b_ref, p_ref, q_ref, k_ref, v_ref, gate_ref):
    xn = _rms(x_ref[...], g_ref[...]).astype(_BF16)

    def proj(lo, width):
        return jnp.dot(xn, w_ref[:, lo:lo + width], preferred_element_type=_F32)

    g0 = POOL_WIDTH + 3 * NA_WIDTH
    for c in range(GATE_WIDTH // 512):
        sl = slice(512 * c, 512 * (c + 1))
        z = proj(g0 + 512 * c, 512) + b_ref[:, sl]
        gate_ref[:, sl] = (0.5 * jnp.tanh(0.5 * z) + 0.5).astype(_BF16)
    p_ref[...] = proj(0, POOL_WIDTH)
    q_ref[...] = (proj(POOL_WIDTH, NA_WIDTH) * QK_SCALE).astype(_BF16)
    k_ref[...] = proj(POOL_WIDTH + NA_WIDTH, NA_WIDTH).astype(_BF16)
    v_ref[...] = proj(POOL_WIDTH + 2 * NA_WIDTH, NA_WIDTH).astype(_BF16)


def _inproj(x2d, norm_g, w_in, b_gate, *, tm):
    n = x2d.shape[0]
    row = lambda w: pl.BlockSpec((tm, w), lambda i: (i, 0))
    return pl.pallas_call(
        _inproj_kernel,
        grid=(n // tm,),
        in_specs=[row(D_MODEL), _const_spec((1, D_MODEL)), _const_spec((D_MODEL, IN_WIDTH)),
                  _const_spec((1, GATE_WIDTH))],
        out_specs=[row(POOL_WIDTH), row(NA_WIDTH), row(NA_WIDTH), row(NA_WIDTH), row(GATE_WIDTH)],
        out_shape=[jax.ShapeDtypeStruct((n, POOL_WIDTH), _F32),
                   jax.ShapeDtypeStruct((n, NA_WIDTH), _BF16),
                   jax.ShapeDtypeStruct((n, NA_WIDTH), _BF16),
                   jax.ShapeDtypeStruct((n, NA_WIDTH), _BF16),
                   jax.ShapeDtypeStruct((n, GATE_WIDTH), _BF16)],
        compiler_params=pltpu.CompilerParams(
            dimension_semantics=("arbitrary",), vmem_limit_bytes=VMEM_LIMIT_BYTES),
        name="inproj",
    )(x2d, norm_g, w_in, b_gate)


def _bias_rows(rpb):
    lo = GRID_W - NA_COLS
    ext = jnp.pad(rpb.astype(_F32) * LOG2_E, ((0, 0), (0, 0), (lo, LANES - lo - (2 * NA_COLS - 1))), mode="edge")
    return ext.reshape(NA_HEADS * N_ROW_OFFSETS, LANES)


def _build_bias_tiles(rext_ref, tb_ref):
    qc = lax.broadcasted_iota(jnp.int32, (GRID_W, LANES), 0)
    lane = lax.broadcasted_iota(jnp.int32, (GRID_W, LANES), 1)
    kc = lane & (GRID_W - 1)
    cs = jnp.clip(qc - NA_COLS // 2, 0, GRID_W - NA_COLS)
    valid = (kc >= cs) & (kc < cs + NA_COLS)
    left = lane < GRID_W
    for h in range(NA_HEADS):
        rows = slice((h % 2) * GRID_W, (h % 2 + 1) * GRID_W)

        def shifted(j, first_lane):
            row = jnp.broadcast_to(rext_ref[pl.ds(h * N_ROW_OFFSETS + j, 1), :], (GRID_W, LANES))
            return pltpu.roll(row, (first_lane + GRID_W + 1) % LANES, 1, stride=1, stride_axis=0)

        for j in range(N_BIAS_TILES):
            tile = jnp.where(left, shifted(j, 0), shifted(j + 1, GRID_W))
            tb_ref[h // 2, j, rows, :] = jnp.where(valid, tile, NEG_INF)


def _attn_kernel(q_ref, k_ref, v_ref, rext_ref, o_ref, s0_ref, s1_ref, p0_ref, p1_ref, m0_ref, m1_ref,
                 vones_ref, tb_ref, *, rows, rb, span):
    blk = pl.program_id(1)
    even_head = lax.broadcasted_iota(jnp.int32, (GRID_W, LANES), 1) < NA_HEAD_DIM
    s_slots, p_slots, m_slots = (s0_ref, s1_ref), (p0_ref, p1_ref), (m0_ref, m1_ref)
    lane_tiles = [slice(m * LANES, (m + 1) * LANES) for m in range(NA_KEYS // LANES)]
    base_row = jnp.clip(blk * rb - NA_ROWS // 2, 0, rows - span)

    @pl.when((pl.program_id(0) == 0) & (blk == 0))
    def _():
        vones_ref[:, :, LANES:] = jnp.ones((HEAD_PAIRS, span * GRID_W, LANES), _BF16)
        _build_bias_tiles(rext_ref, tb_ref)

    v0 = pl.multiple_of(base_row * GRID_W, GRID_W)
    for hp in ALL_HEAD_PAIRS:
        vones_ref[hp, :, :LANES] = v_ref[0, pl.ds(v0, span * GRID_W), hp * LANES:(hp + 1) * LANES]

    def key_rows(j):
        r = blk * rb + j
        rs = jnp.clip(r - NA_ROWS // 2, 0, rows - NA_ROWS)
        return rs, r - rs

    def scores(j, slot, hps=ALL_HEAD_PAIRS):
        rs, delta = key_rows(j)
        k0 = pl.multiple_of(rs * GRID_W, GRID_W)
        q0 = pl.multiple_of(j * GRID_W, GRID_W)
        for hp in hps:
            ls = slice(hp * LANES, (hp + 1) * LANES)
            qp = q_ref[0, pl.ds(q0, GRID_W), ls]
            zero = jnp.zeros_like(qp)
            q2 = jnp.concatenate([jnp.where(even_head, qp, zero), jnp.where(even_head, zero, qp)], axis=0)
            kp = k_ref[0, pl.ds(k0, NA_KEYS), ls]
            qk = lax.dot_general(q2, kp, (((1,), (1,)), ((), ())), preferred_element_type=_F32)
            s = [qk[:, lt] + tb_ref[hp, 2 * m - delta + (NA_ROWS - 1)] for m, lt in enumerate(lane_tiles)]
            for sm, lt in zip(s, lane_tiles):
                s_slots[slot][hp, :, lt] = sm
            mx = jnp.max(functools.reduce(jnp.maximum, s), axis=-1, keepdims=True)
            m_slots[slot][hp] = jnp.broadcast_to(mx, (2 * GRID_W, LANES))

    def softmax(j, slot, hps=ALL_HEAD_PAIRS):
        for hp in hps:
            mx = m_slots[slot][hp]
            for lt in lane_tiles:
                p_slots[slot][hp, :, lt] = jnp.exp2(s_slots[slot][hp, :, lt] - mx).astype(_BF16)

    def weighted_values(j, slot, hps=ALL_HEAD_PAIRS):
        rs, _ = key_rows(j)
        k0 = pl.multiple_of((rs - base_row) * GRID_W, GRID_W)
        q0 = pl.multiple_of(j * GRID_W, GRID_W)
        for hp in hps:
            ls = slice(hp * LANES, (hp + 1) * LANES)
            vo = vones_ref[hp, pl.ds(k0, NA_KEYS), :]
            o2l = jnp.dot(p_slots[slot][hp], vo, preferred_element_type=_F32)
            o2 = o2l[:, :LANES] / o2l[:, LANES:]
            out = jnp.where(even_head, o2[:GRID_W], o2[GRID_W:])
            o_ref[0, pl.ds(q0, GRID_W), ls] = out.astype(_BF16)

    for step in range(rb + 2):
        if step < rb:
            scores(step, step % 2)
        if 1 <= step <= rb:
            softmax(step - 1, (step - 1) % 2)
        if step >= 2:
            weighted_values(step - 2, (step - 2) % 2)


def _attention(q, k, v, bias_rows, *, rb):
    b, s, _ = q.shape
    rows = s // GRID_W
    assert rows % rb == 0
    tq = rb * GRID_W
    qspec = pl.BlockSpec((1, tq, NA_WIDTH), lambda bi, i: (bi, i, 0))
    kvspec = pl.BlockSpec((1, s, NA_WIDTH), lambda bi, i: (bi, 0, 0))
    pair_rows = 2 * GRID_W
    span = min(rb + NA_ROWS, rows)
    return pl.pallas_call(
        functools.partial(_attn_kernel, rows=rows, rb=rb, span=span),
        grid=(b, rows // rb),
        in_specs=[qspec, kvspec, kvspec, _const_spec(bias_rows.shape)],
        out_specs=qspec,
        out_shape=jax.ShapeDtypeStruct((b, s, NA_WIDTH), _BF16),
        scratch_shapes=[pltpu.VMEM((HEAD_PAIRS, pair_rows, NA_KEYS), _F32)] * 2
        + [pltpu.VMEM((HEAD_PAIRS, pair_rows, NA_KEYS), _BF16)] * 2
        + [pltpu.VMEM((HEAD_PAIRS, pair_rows, LANES), _F32)] * 2
        + [pltpu.VMEM((HEAD_PAIRS, span * GRID_W, 2 * LANES), _BF16),
           pltpu.VMEM((HEAD_PAIRS, N_BIAS_TILES, pair_rows, LANES), _F32)],
        compiler_params=pltpu.CompilerParams(
            dimension_semantics=("arbitrary", "arbitrary"), vmem_limit_bytes=ATT_VMEM_LIMIT_BYTES),
        name="natten",
    )(q, k, v, bias_rows)


def _mix_mlp_kernel(x_ref, p_ref, pprev_ref, pnext_ref, gate_ref, att_ref,
                    wgrp_ref, pscale_ref, wpool_ref, wna_ref, wout_ref,
                    nmlp_ref, w1_ref, w2_ref, nfin_ref, out_ref, ext_ref, *, tm, sub, seq):
    s0 = (pl.program_id(0) * tm) % seq
    ext_ref[0:POOL_HALO, :] = jnp.where(s0 == 0, 0.0, pprev_ref[...])
    ext_ref[POOL_HALO:POOL_HALO + tm, :] = p_ref[...]
    ext_ref[POOL_HALO + tm:, :] = jnp.where(s0 + tm == seq, 0.0, pnext_ref[...])

    def window_mean(tot, r0, w):
        h = w // 2
        mean = tot * (1.0 / w)

        def cut_rows(lo):
            pos = ((s0 + r0 + lo).astype(_F32)
                   + lax.broadcasted_iota(jnp.int32, (POOL_HALO, POOL_GROUP_W), 0).astype(_F32))
            cnt = jnp.minimum(pos + float(h), float(seq)) - jnp.maximum(pos - float(h), 0.0)
            return tot[lo:lo + POOL_HALO] / cnt

        if r0 == 0:
            mean = jnp.concatenate([cut_rows(0), mean[POOL_HALO:]], axis=0)
        if r0 + sub == tm:
            mean = jnp.concatenate([mean[:sub - POOL_HALO], cut_rows(sub - POOL_HALO)], axis=0)
        return mean

    def mixer(r0):
        rows = slice(r0, r0 + sub)
        na_out = jnp.dot(att_ref[rows, :], wna_ref[...], preferred_element_type=_F32)
        mixed = []
        for g, w in enumerate(POOL_WINDOWS):
            ls = slice(g * POOL_GROUP_W, (g + 1) * POOL_GROUP_W)
            h = w // 2
            run = ext_ref[POOL_HALO + r0 - h:POOL_HALO + r0 + sub + h, ls]
            width = 1
            while 2 * width < w:
                n = run.shape[0] - width
                run = run[:n] + run[width:]
                width *= 2
            tot = run[:sub] + run[width:width + sub]
            pooled = window_mean(tot, r0, w) - p_ref[rows, ls]
            mixed.append(jnp.dot(pooled.astype(_BF16), wgrp_ref[g], preferred_element_type=_F32))
        mixed = jnp.concatenate(mixed, axis=1) * pscale_ref[...]
        pool_out = jnp.dot(mixed.astype(_BF16), wpool_ref[...], preferred_element_type=_F32)
        merged = gate_ref[rows, :D_MODEL] * pool_out + gate_ref[rows, D_MODEL:] * na_out
        return x_ref[rows, :] + jnp.dot(merged.astype(_BF16), wout_ref[...], preferred_element_type=_F32)

    def mlp(r0, x1):
        xn = _rms(x1, nmlp_ref[...]).astype(_BF16)
        y = x1
        for c in range(D_FF // FF_CHUNK):
            cs = slice(c * FF_CHUNK, (c + 1) * FF_CHUNK)
            hid = jnp.square(jnp.maximum(jnp.dot(xn, w1_ref[:, cs], preferred_element_type=_F32), 0.0))
            y = y + jnp.dot(hid.astype(_BF16), w2_ref[cs, :], preferred_element_type=_F32)
        out_ref[r0:r0 + sub, :] = _rms(y, nfin_ref[...])

    starts = range(0, tm, sub)
    x1s = [mixer(r0) for r0 in starts]
    for r0, x1 in zip(starts, x1s):
        mlp(r0, x1)


def _mix_mlp(x2d, p, gates, att, wgrp, pscale, wpool, wna, wout, nmlp, w1, w2, nfin, *, tm, seq):
    n = x2d.shape[0]
    assert tm % MLP_SUB_TILE == 0 and seq % tm == 0
    hb = tm // POOL_HALO
    n_hb = n // POOL_HALO
    row = lambda w: pl.BlockSpec((tm, w), lambda i: (i, 0))
    prev = pl.BlockSpec((POOL_HALO, POOL_WIDTH), lambda i: (jnp.maximum(i * hb - 1, 0), 0))
    nxt = pl.BlockSpec((POOL_HALO, POOL_WIDTH), lambda i: (jnp.minimum((i + 1) * hb, n_hb - 1), 0))
    return pl.pallas_call(
        functools.partial(_mix_mlp_kernel, tm=tm, sub=MLP_SUB_TILE, seq=seq),
        grid=(n // tm,),
        in_specs=[row(D_MODEL), row(POOL_WIDTH), prev, nxt, row(GATE_WIDTH), row(NA_WIDTH),
                  _const_spec(wgrp.shape), _const_spec(pscale.shape), _const_spec(wpool.shape),
                  _const_spec(wna.shape), _const_spec(wout.shape), _const_spec(nmlp.shape),
                  _const_spec(w1.shape), _const_spec(w2.shape), _const_spec(nfin.shape)],
        out_specs=row(D_MODEL),
        out_shape=jax.ShapeDtypeStruct((n, D_MODEL), _F32),
        scratch_shapes=[pltpu.VMEM((tm + 2 * POOL_HALO, POOL_WIDTH), _F32)],
        compiler_params=pltpu.CompilerParams(
            dimension_semantics=("arbitrary",), vmem_limit_bytes=VMEM_LIMIT_BYTES),
        name="mix_mlp",
    )(x2d, p, p, p, gates, att, wgrp, pscale, wpool, wna, wout, nmlp, w1, w2, nfin)


def _layer(x, wts, bias_rows, *, tm_in, tm_mlp, rb):
    b, s, d = x.shape
    x2d = x.reshape(b * s, d)
    p, q, k, v, gates = _inproj(x2d, wts["norm_mix"], wts["w_in"], wts["b_gate"], tm=tm_in)
    shp = (b, s, NA_WIDTH)
    att = _attention(q.reshape(shp), k.reshape(shp), v.reshape(shp), bias_rows, rb=rb)
    y = _mix_mlp(x2d, p, gates, att.reshape(b * s, NA_WIDTH), wts["w_pool_grp"], wts["pool_scale"],
                 wts["w_pool_proj"], wts["w_na_proj"], wts["w_out"], wts["norm_mlp"],
                 wts["w_ff1"], wts["w_ff2"], wts["norm_final"], tm=tm_mlp, seq=s)
    return y.reshape(b, s, d)


def kernel(x_prompt, x_sample, norm_mix, w_in, b_gate, w_pool_grp, pool_scale, w_pool_proj, rpb,
           w_na_proj, w_out, norm_mlp, w_ff1, w_ff2, norm_final):
    depth = norm_mix.shape[0]
    assert depth == 1
    l = 0
    wts = {
        "norm_mix": norm_mix[l].reshape(1, D_MODEL),
        "w_in": w_in[l].astype(_BF16),
        "b_gate": b_gate[l].reshape(1, GATE_WIDTH),
        "w_pool_grp": w_pool_grp[l].astype(_BF16),
        "pool_scale": pool_scale[l].reshape(1, POOL_WIDTH),
        "w_pool_proj": w_pool_proj[l].astype(_BF16),
        "w_na_proj": w_na_proj[l].astype(_BF16),
        "w_out": w_out[l].astype(_BF16),
        "norm_mlp": norm_mlp[l].reshape(1, D_MODEL),
        "w_ff1": w_ff1[l].astype(_BF16),
        "w_ff2": w_ff2[l].astype(_BF16),
        "norm_final": norm_final.reshape(1, D_MODEL),
    }
    bias_rows = _bias_rows(rpb[l])
    run = functools.partial(_layer, wts=wts, bias_rows=bias_rows, tm_in=1024, tm_mlp=512, rb=32)
    return (run(x_prompt), run(x_sample))
```

```python
import functools

import jax
import jax.numpy as jnp
from jax import lax
from jax.experimental import pallas as pl
from jax.experimental.pallas import tpu as pltpu

D_MODEL = 1024
GRID_W = 64
POOL_WIDTH = 512
POOL_GROUPS = 4
POOL_GROUP_W = POOL_WIDTH // POOL_GROUPS
POOL_WINDOWS = (2, 4, 8, 16)
POOL_HALO = max(POOL_WINDOWS) // 2
NA_HEADS = 8
NA_HEAD_DIM = 64
NA_WIDTH = NA_HEADS * NA_HEAD_DIM
NA_ROWS = 8
NA_COLS = 16
NA_KEYS = NA_ROWS * GRID_W
HEAD_PAIRS = NA_HEADS // 2
ALL_HEAD_PAIRS = tuple(range(HEAD_PAIRS))
N_ROW_OFFSETS = 2 * NA_ROWS - 1
N_BIAS_TILES = N_ROW_OFFSETS - 1
D_FF = 4 * D_MODEL
FF_CHUNK = 1024
MLP_SUB_TILE = 256
GATE_WIDTH = 2 * D_MODEL
IN_WIDTH = POOL_WIDTH + 3 * NA_WIDTH + GATE_WIDTH
RMS_EPS = 1e-6
NEG_INF = -1e30
LOG2_E = 1.4426950408889634
QK_SCALE = NA_HEAD_DIM ** -0.5 * LOG2_E

LANES = 128
VMEM_LIMIT_BYTES = 56 * 1024 * 1024

_BF16 = jnp.bfloat16
_F32 = jnp.float32


def _rms(x, g):
    var = jnp.mean(x * x, axis=-1, keepdims=True)
    return x * lax.rsqrt(var + RMS_EPS) * g


def _const_spec(shape):
    zeros = (0,) * len(shape)
    return pl.BlockSpec(shape, lambda *_: zeros, pipeline_mode=pl.Buffered(1))


def _inproj_kernel(x_ref, g_ref, w_ref, b_ref, p_ref, q_ref, k_ref, v_ref, gate_ref):
    xn = _rms(x_ref[...], g_ref[...]).astype(_BF16)

    def proj(lo, width):
        return jnp.dot(xn, w_ref[:, lo:lo + width], preferred_element_type=_F32)

    g0 = POOL_WIDTH + 3 * NA_WIDTH
    for c in range(GATE_WIDTH // 512):
        sl = slice(512 * c, 512 * (c + 1))
        z = proj(g0 + 512 * c, 512) + b_ref[:, sl]
        gate_ref[:, sl] = (0.5 * jnp.tanh(0.5 * z) + 0.5).astype(_BF16)
    p_ref[...] = proj(0, POOL_WIDTH)
    q_ref[...] = (proj(POOL_WIDTH, NA_WIDTH) * QK_SCALE).astype(_BF16)
    k_ref[...] = proj(POOL_WIDTH + NA_WIDTH, NA_WIDTH).astype(_BF16)
    v_ref[...] = proj(POOL_WIDTH + 2 * NA_WIDTH, NA_WIDTH).astype(_BF16)


def _inproj(x2d, norm_g, w_in, b_gate, *, tm):
    n = x2d.shape[0]
    row = lambda w: pl.BlockSpec((tm, w), lambda i: (i, 0))
    return pl.pallas_call(
        _inproj_kernel,
        grid=(n // tm,),
        in_specs=[row(D_MODEL), _const_spec((1, D_MODEL)), _const_spec((D_MODEL, IN_WIDTH)),
                  _const_spec((1, GATE_WIDTH))],
        out_specs=[row(POOL_WIDTH), row(NA_WIDTH), row(NA_WIDTH), row(NA_WIDTH), row(GATE_WIDTH)],
        out_shape=[jax.ShapeDtypeStruct((n, POOL_WIDTH), _F32),
                   jax.ShapeDtypeStruct((n, NA_WIDTH), _BF16),
                   jax.ShapeDtypeStruct((n, NA_WIDTH), _BF16),
                   jax.ShapeDtypeStruct((n, NA_WIDTH), _BF16),
                   jax.ShapeDtypeStruct((n, GATE_WIDTH), _BF16)],
        compiler_params=pltpu.CompilerParams(
            dimension_semantics=("arbitrary",), vmem_limit_bytes=VMEM_LIMIT_BYTES),
        name="inproj",
    )(x2d, norm_g, w_in, b_gate)


def _bias_rows(rpb):
    lo = GRID_W - NA_COLS
    ext = jnp.pad(rpb.astype(_F32) * LOG2_E, ((0, 0), (0, 0), (lo, LANES - lo - (2 * NA_COLS - 1))), mode="edge")
    return ext.reshape(NA_HEADS * N_ROW_OFFSETS, LANES)


def _build_bias_tiles(rext_ref, tb_ref):
    qc = lax.broadcasted_iota(jnp.int32, (GRID_W, LANES), 0)
    lane = lax.broadcasted_iota(jnp.int32, (GRID_W, LANES), 1)
    kc = lane & (GRID_W - 1)
    cs = jnp.clip(qc - NA_COLS // 2, 0, GRID_W - NA_COLS)
    valid = (kc >= cs) & (kc < cs + NA_COLS)
    left = lane < GRID_W
    for h in range(NA_HEADS):
        rows = slice((h % 2) * GRID_W, (h % 2 + 1) * GRID_W)

        def shifted(j, first_lane):
            row = jnp.broadcast_to(rext_ref[pl.ds(h * N_ROW_OFFSETS + j, 1), :], (GRID_W, LANES))
            return pltpu.roll(row, (first_lane + GRID_W + 1) % LANES, 1, stride=1, stride_axis=0)

        for j in range(N_BIAS_TILES):
            tile = jnp.where(left, shifted(j, 0), shifted(j + 1, GRID_W))
            tb_ref[h // 2, j, rows, :] = jnp.where(valid, tile, NEG_INF)


def _attn_kernel(q_ref, k_ref, v_ref, rext_ref, o_ref, s0_ref, s1_ref, p0_ref, p1_ref, m0_ref, m1_ref,
                 vones_ref, tb_ref, *, rows, rb, span):
    blk = pl.program_id(1)
    even_head = lax.broadcasted_iota(jnp.int32, (GRID_W, LANES), 1) < NA_HEAD_DIM
    s_slots, p_slots, m_slots = (s0_ref, s1_ref), (p0_ref, p1_ref), (m0_ref, m1_ref)
    lane_tiles = [slice(m * LANES, (m + 1) * LANES) for m in range(NA_KEYS // LANES)]
    base_row = jnp.clip(blk * rb - NA_ROWS // 2, 0, rows - span)

    @pl.when((pl.program_id(0) == 0) & (blk == 0))
    def _():
        vones_ref[:, :, LANES:] = jnp.ones((HEAD_PAIRS, span * GRID_W, LANES), _BF16)
        _build_bias_tiles(rext_ref, tb_ref)

    v0 = pl.multiple_of(base_row * GRID_W, GRID_W)
    for hp in ALL_HEAD_PAIRS:
        vones_ref[hp, :, :LANES] = v_ref[0, pl.ds(v0, span * GRID_W), hp * LANES:(hp + 1) * LANES]

    def key_rows(j):
        r = blk * rb + j
        rs = jnp.clip(r - NA_ROWS // 2, 0, rows - NA_ROWS)
        return rs, r - rs

    def scores(j, slot, hps=ALL_HEAD_PAIRS):
        rs, delta = key_rows(j)
        k0 = pl.multiple_of(rs * GRID_W, GRID_W)
        q0 = pl.multiple_of(j * GRID_W, GRID_W)
        for hp in hps:
            ls = slice(hp * LANES, (hp + 1) * LANES)
            qp = q_ref[0, pl.ds(q0, GRID_W), ls]
            zero = jnp.zeros_like(qp)
            q2 = jnp.concatenate([jnp.where(even_head, qp, zero), jnp.where(even_head, zero, qp)], axis=0)
            kp = k_ref[0, pl.ds(k0, NA_KEYS), ls]
            qk = lax.dot_general(q2, kp, (((1,), (1,)), ((), ())), preferred_element_type=_F32)
            s = [qk[:, lt] + tb_ref[hp, 2 * m - delta + (NA_ROWS - 1)] for m, lt in enumerate(lane_tiles)]
            for sm, lt in zip(s, lane_tiles):
                s_slots[slot][hp, :, lt] = sm
            mx = jnp.max(functools.reduce(jnp.maximum, s), axis=-1, keepdims=True)
            m_slots[slot][hp] = jnp.broadcast_to(mx, (2 * GRID_W, LANES))

    def softmax(j, slot, hps=ALL_HEAD_PAIRS):
        for hp in hps:
            mx = m_slots[slot][hp]
            for lt in lane_tiles:
                p_slots[slot][hp, :, lt] = jnp.exp2(s_slots[slot][hp, :, lt] - mx).astype(_BF16)

    def weighted_values(j, slot, hps=ALL_HEAD_PAIRS):
        rs, _ = key_rows(j)
        k0 = pl.multiple_of((rs - base_row) * GRID_W, GRID_W)
        q0 = pl.multiple_of(j * GRID_W, GRID_W)
        for hp in hps:
            ls = slice(hp * LANES, (hp + 1) * LANES)
            vo = vones_ref[hp, pl.ds(k0, NA_KEYS), :]
            o2l = jnp.dot(p_slots[slot][hp], vo, preferred_element_type=_F32)
            o2 = o2l[:, :LANES] / o2l[:, LANES:]
            out = jnp.where(even_head, o2[:GRID_W], o2[GRID_W:])
            o_ref[0, pl.ds(q0, GRID_W), ls] = out.astype(_BF16)

    for step in range(rb + 2):
        if step < rb:
            scores(step, step % 2)
        if 1 <= step <= rb:
            softmax(step - 1, (step - 1) % 2)
        if step >= 2:
            weighted_values(step - 2, (step - 2) % 2)


def _attention(q, k, v, bias_rows, *, rb):
    b, s, _ = q.shape
    rows = s // GRID_W
    assert rows % rb == 0
    tq = rb * GRID_W
    qspec = pl.BlockSpec((1, tq, NA_WIDTH), lambda bi, i: (bi, i, 0))
    kvspec = pl.BlockSpec((1, s, NA_WIDTH), lambda bi, i: (bi, 0, 0))
    pair_rows = 2 * GRID_W
    span = min(rb + NA_ROWS, rows)
    return pl.pallas_call(
        functools.partial(_attn_kernel, rows=rows, rb=rb, span=span),
        grid=(b, rows // rb),
        in_specs=[qspec, kvspec, kvspec, _const_spec(bias_rows.shape)],
        out_specs=qspec,
        out_shape=jax.ShapeDtypeStruct((b, s, NA_WIDTH), _BF16),
        scratch_shapes=[pltpu.VMEM((HEAD_PAIRS, pair_rows, NA_KEYS), _F32)] * 2
        + [pltpu.VMEM((HEAD_PAIRS, pair_rows, NA_KEYS), _BF16)] * 2
        + [pltpu.VMEM((HEAD_PAIRS, pair_rows, LANES), _F32)] * 2
        + [pltpu.VMEM((HEAD_PAIRS, span * GRID_W, 2 * LANES), _BF16),
           pltpu.VMEM((HEAD_PAIRS, N_BIAS_TILES, pair_rows, LANES), _F32)],
        compiler_params=pltpu.CompilerParams(
            dimension_semantics=("arbitrary", "arbitrary"), vmem_limit_bytes=VMEM_LIMIT_BYTES),
        name="natten",
    )(q, k, v, bias_rows)


def _mix_mlp_kernel(x_ref, p_ref, pprev_ref, pnext_ref, gate_ref, att_ref,
                    wgrp_ref, pscale_ref, wpool_ref, wna_ref, wout_ref,
                    nmlp_ref, w1_ref, w2_ref, nfin_ref, out_ref, *, tm, sub, seq):
    s0 = (pl.program_id(0) * tm) % seq
    before = jnp.where(s0 == 0, 0.0, pprev_ref[...])
    after = jnp.where(s0 + tm == seq, 0.0, pnext_ref[...])

    def window_mean(tot, r0, w):
        h = w // 2
        mean = tot * (1.0 / w)

        def cut_rows(lo):
            pos = ((s0 + r0 + lo).astype(_F32)
                   + lax.broadcasted_iota(jnp.int32, (POOL_HALO, POOL_GROUP_W), 0).astype(_F32))
            cnt = jnp.minimum(pos + float(h), float(seq)) - jnp.maximum(pos - float(h), 0.0)
            return tot[lo:lo + POOL_HALO] / cnt

        if r0 == 0:
            mean = jnp.concatenate([cut_rows(0), mean[POOL_HALO:]], axis=0)
        if r0 + sub == tm:
            mean = jnp.concatenate([mean[:sub - POOL_HALO], cut_rows(sub - POOL_HALO)], axis=0)
        return mean

    def mixer(r0):
        rows = slice(r0, r0 + sub)
        mixed = []
        for g, w in enumerate(POOL_WINDOWS):
            ls = slice(g * POOL_GROUP_W, (g + 1) * POOL_GROUP_W)
            h = w // 2
            lo = before[:, ls] if r0 == 0 else p_ref[r0 - POOL_HALO:r0, ls]
            hi = after[:, ls] if r0 + sub == tm else p_ref[r0 + sub:r0 + sub + POOL_HALO, ls]
            run = jnp.concatenate([lo, p_ref[rows, ls], hi], axis=0)[POOL_HALO - h:POOL_HALO + sub + h]
            width = 1
            while 2 * width < w:
                n = run.shape[0] - width
                run = run[:n] + run[width:]
                width *= 2
            tot = run[:sub] + run[width:width + sub]
            pooled = window_mean(tot, r0, w) - p_ref[rows, ls]
            mixed.append(jnp.dot(pooled.astype(_BF16), wgrp_ref[g], preferred_element_type=_F32))
        mixed = jnp.concatenate(mixed, axis=1) * pscale_ref[...]
        pool_out = jnp.dot(mixed.astype(_BF16), wpool_ref[...], preferred_element_type=_F32)
        na_out = jnp.dot(att_ref[rows, :], wna_ref[...], preferred_element_type=_F32)
        merged = gate_ref[rows, :D_MODEL] * pool_out + gate_ref[rows, D_MODEL:] * na_out
        return x_ref[rows, :] + jnp.dot(merged.astype(_BF16), wout_ref[...], preferred_element_type=_F32)

    def mlp(r0, x1):
        xn = _rms(x1, nmlp_ref[...]).astype(_BF16)
        y = x1
        for c in range(D_FF // FF_CHUNK):
            cs = slice(c * FF_CHUNK, (c + 1) * FF_CHUNK)
            hid = jnp.square(jnp.maximum(jnp.dot(xn, w1_ref[:, cs], preferred_element_type=_F32), 0.0))
            y = y + jnp.dot(hid.astype(_BF16), w2_ref[cs, :], preferred_element_type=_F32)
        out_ref[r0:r0 + sub, :] = _rms(y, nfin_ref[...])

    starts = range(0, tm, sub)
    x1s = [mixer(r0) for r0 in starts]
    for r0, x1 in zip(starts, x1s):
        mlp(r0, x1)


def _mix_mlp(x2d, p, gates, att, wgrp, pscale, wpool, wna, wout, nmlp, w1, w2, nfin, *, tm, seq):
    n = x2d.shape[0]
    assert tm % MLP_SUB_TILE == 0 and seq % tm == 0
    hb = tm // POOL_HALO
    n_hb = n // POOL_HALO
    row = lambda w: pl.BlockSpec((tm, w), lambda i: (i, 0))
    prev = pl.BlockSpec((POOL_HALO, POOL_WIDTH), lambda i: (jnp.maximum(i * hb - 1, 0), 0))
    nxt = pl.BlockSpec((POOL_HALO, POOL_WIDTH), lambda i: (jnp.minimum((i + 1) * hb, n_hb - 1), 0))
    return pl.pallas_call(
        functools.partial(_mix_mlp_kernel, tm=tm, sub=MLP_SUB_TILE, seq=seq),
        grid=(n // tm,),
        in_specs=[row(D_MODEL), row(POOL_WIDTH), prev, nxt, row(GATE_WIDTH), row(NA_WIDTH),
                  _const_spec(wgrp.shape), _const_spec(pscale.shape), _const_spec(wpool.shape),
                  _const_spec(wna.shape), _const_spec(wout.shape), _const_spec(nmlp.shape),
                  _const_spec(w1.shape), _const_spec(w2.shape), _const_spec(nfin.shape)],
        out_specs=row(D_MODEL),
        out_shape=jax.ShapeDtypeStruct((n, D_MODEL), _F32),
        compiler_params=pltpu.CompilerParams(
            dimension_semantics=("arbitrary",), vmem_limit_bytes=VMEM_LIMIT_BYTES),
        name="mix_mlp",
    )(x2d, p, p, p, gates, att, wgrp, pscale, wpool, wna, wout, nmlp, w1, w2, nfin)


def _layer(x, wts, bias_rows, *, tm_in, tm_mlp, rb):
    b, s, d = x.shape
    x2d = x.reshape(b * s, d)
    p, q, k, v, gates = _inproj(x2d, wts["norm_mix"], wts["w_in"], wts["b_gate"], tm=tm_in)
    shp = (b, s, NA_WIDTH)
    att = _attention(q.reshape(shp), k.reshape(shp), v.reshape(shp), bias_rows, rb=rb)
    y = _mix_mlp(x2d, p, gates, att.reshape(b * s, NA_WIDTH), wts["w_pool_grp"], wts["pool_scale"],
                 wts["w_pool_proj"], wts["w_na_proj"], wts["w_out"], wts["norm_mlp"],
                 wts["w_ff1"], wts["w_ff2"], wts["norm_final"], tm=tm_mlp, seq=s)
    return y.reshape(b, s, d)


def kernel(x_prompt, x_sample, norm_mix, w_in, b_gate, w_pool_grp, pool_scale, w_pool_proj, rpb,
           w_na_proj, w_out, norm_mlp, w_ff1, w_ff2, norm_final):
    depth = norm_mix.shape[0]
    assert depth == 1
    l = 0
    wts = {
        "norm_mix": norm_mix[l].reshape(1, D_MODEL),
        "w_in": w_in[l].astype(_BF16),
        "b_gate": b_gate[l].reshape(1, GATE_WIDTH),
        "w_pool_grp": w_pool_grp[l].astype(_BF16),
        "pool_scale": pool_scale[l].reshape(1, POOL_WIDTH),
        "w_pool_proj": w_pool_proj[l].astype(_BF16),
        "w_na_proj": w_na_proj[l].astype(_BF16),
        "w_out": w_out[l].astype(_BF16),
        "norm_mlp": norm_mlp[l].reshape(1, D_MODEL),
        "w_ff1": w_ff1[l].astype(_BF16),
        "w_ff2": w_ff2[l].astype(_BF16),
        "norm_final": norm_final.reshape(1, D_MODEL),
    }
    bias_rows = _bias_rows(rpb[l])
    run = functools.partial(_layer, wts=wts, bias_rows=bias_rows, tm_in=1024, tm_mlp=512, rb=32)
    return (run(x_prompt), run(x_sample))
```
